```python
import math
import jax, jax.numpy as jnp
from jax import lax
import numpy as np

D_MODEL = 1024
BATCH = 16
SEQ = 4096
DEPTH = 2

CONV_WIDTH = D_MODEL // 2
CONV_K = 31
DN_DK = 128
DN_DV = 128
DN_HEADS = D_MODEL // 128
DN_SHORT_K = 4
DN_CHUNK = 64
SG_WIDTH = D_MODEL // 2
SG_GROUPS = 4
SG_CHUNK = 128
N_BRANCH = 3
NORM_EPS = 1e-6

SPLIT_SIZES = (
    CONV_WIDTH,
    CONV_WIDTH,
    CONV_WIDTH,
    DN_HEADS * (2 * DN_DK + DN_DV),
    DN_HEADS * DN_DV,
    DN_HEADS,
    DN_HEADS,
    SG_WIDTH,
    SG_WIDTH,
    SG_WIDTH,
    N_BRANCH * D_MODEL,
)
N_IN_COLS = sum(SPLIT_SIZES)

kernel_name = "hybrid_conv_deltanet_gmlp_gated_merge"


def _rmsnorm(x, g):
    xf = x.astype(jnp.float32)
    y = xf * lax.rsqrt(jnp.mean(xf * xf, axis=-1, keepdims=True) + NORM_EPS)
    return (y * g.astype(jnp.float32)).astype(x.dtype)


def _layernorm(x, g, b):
    xf = x.astype(jnp.float32)
    mu = jnp.mean(xf, axis=-1, keepdims=True)
    var = jnp.mean(jnp.square(xf - mu), axis=-1, keepdims=True)
    y = (xf - mu) * lax.rsqrt(var + NORM_EPS)
    return (y * g.astype(jnp.float32) + b.astype(jnp.float32)).astype(x.dtype)


def _l2norm(x):
    xf = x.astype(jnp.float32)
    return xf * lax.rsqrt(jnp.sum(xf * xf, axis=-1, keepdims=True) + NORM_EPS)


def _causal_dwconv(x, w):
    k, c = w.shape
    xp = jnp.pad(x, ((0, 0), (k - 1, 0), (0, 0)))
    return lax.conv_general_dilated(
        xp, w.astype(x.dtype)[:, None, :], window_strides=(1,), padding="VALID",
        dimension_numbers=("NWC", "WIO", "NWC"), feature_group_count=c)


def _gated_delta_rule(q, k, v, beta, g):
    bsz, t, h, dk = q.shape
    dv = v.shape[-1]
    c = DN_CHUNK
    n = t // c

    def chunks(a):
        a = a.reshape((bsz, n, c, h) + a.shape[3:])
        return jnp.moveaxis(a, 3, 1)

    q, k, v, beta, g = chunks(q), chunks(k), chunks(v), chunks(beta), chunks(g)
    gc = jnp.cumsum(g, axis=-1)
    diff = gc[..., :, None] - gc[..., None, :]
    incl = jnp.tril(jnp.ones((c, c), dtype=bool))
    strict = jnp.tril(jnp.ones((c, c), dtype=bool), k=-1)
    gamma_incl = jnp.exp(jnp.where(incl, diff, -jnp.inf))
    gamma_strict = jnp.exp(jnp.where(strict, diff, -jnp.inf))

    kk = jnp.einsum("bhnid,bhnjd->bhnij", k, k)
    a_mat = jnp.eye(c, dtype=jnp.float32) + beta[..., :, None] * kk * gamma_strict
    rhs = jnp.concatenate([v * beta[..., None],
                           k * (beta * jnp.exp(gc))[..., None]], axis=-1)
    sol = lax.linalg.triangular_solve(a_mat, rhs, left_side=True, lower=True,
                                      unit_diagonal=True)
    u, w = sol[..., :dv], sol[..., dv:]

    qk = jnp.einsum("bhnid,bhnjd->bhnij", q, k) * gamma_incl
    q_dec = q * jnp.exp(gc)[..., None]
    k_dec = k * jnp.exp(gc[..., -1:] - gc)[..., None]
    d_last = jnp.exp(gc[..., -1])

    def step(s, xs):
        u_c, w_c, qd_c, qk_c, kd_c, dl_c = xs
        v_new = u_c - jnp.einsum("bhcd,bhde->bhce", w_c, s)
        o_c = (jnp.einsum("bhcd,bhde->bhce", qd_c, s)
               + jnp.einsum("bhij,bhje->bhie", qk_c, v_new))
        s = dl_c[..., None, None] * s + jnp.einsum("bhcd,bhce->bhde", kd_c, v_new)
        return s, o_c

    xs = tuple(jnp.moveaxis(a, 2, 0) for a in (u, w, q_dec, qk, k_dec, d_last))
    s0 = jnp.zeros((bsz, h, dk, dv), jnp.float32)
    _, o = lax.scan(step, s0, xs)
    o = jnp.transpose(o, (1, 0, 3, 2, 4))
    return o.reshape(bsz, t, h, dv)


def _layer(x, norm_g, w_in, a_dw, a_dw_b, a_ln_g, a_ln_b, a_proj,
           b_conv, b_a_log, b_dt_bias, b_onorm_g, b_proj,
           c_ln_g, c_ln_b, c_ws, c_bs, c_proj, w_out):
    bsz, t, _ = x.shape
    h = _rmsnorm(x, norm_g)
    proj = h @ w_in
    idx = [int(i) for i in np.cumsum(SPLIT_SIZES)[:-1]]
    (a_val, a_glu, a_z, qkv, b_z, b_beta, b_alpha,
     c_u, c_v, c_z, gate_logits) = jnp.split(proj, idx, axis=-1)

    a = a_val * jax.nn.sigmoid(a_glu)
    a = _causal_dwconv(a, a_dw) + a_dw_b
    a = _layernorm(a, a_ln_g, a_ln_b)
    y_a = jax.nn.silu(a) * jax.nn.silu(a_z)

    qkv = jax.nn.silu(_causal_dwconv(qkv, b_conv))
    q, k, v = jnp.split(qkv, [DN_HEADS * DN_DK, 2 * DN_HEADS * DN_DK], axis=-1)
    q = _l2norm(q.reshape(bsz, t, DN_HEADS, DN_DK)) * (DN_DK ** -0.5)
    k = _l2norm(k.reshape(bsz, t, DN_HEADS, DN_DK))
    v = v.reshape(bsz, t, DN_HEADS, DN_DV).astype(jnp.float32)
    beta = jax.nn.sigmoid(b_beta.astype(jnp.float32))
    g = -jnp.exp(b_a_log.astype(jnp.float32)) * jax.nn.softplus(
        b_alpha.astype(jnp.float32) + b_dt_bias.astype(jnp.float32))
    o = _gated_delta_rule(q, k, v, beta, g)
    o = _rmsnorm(o, b_onorm_g)
    z = jax.nn.silu(b_z.reshape(bsz, t, DN_HEADS, DN_DV).astype(jnp.float32))
    y_b = (o * z).reshape(bsz, t, DN_HEADS * DN_DV).astype(x.dtype)

    u = jax.nn.gelu(c_u)
    vs = _layernorm(jax.nn.gelu(c_v), c_ln_g, c_ln_b)
    vs = vs.reshape(bsz, t // SG_CHUNK, SG_CHUNK, SG_GROUPS, SG_WIDTH // SG_GROUPS)
    ws = jnp.tril(c_ws)
    mixed = jnp.einsum("gij,bnjgc->bnigc", ws, vs) + c_bs.T[:, :, None]
    y_c = u * mixed.reshape(bsz, t, SG_WIDTH) * jax.nn.silu(c_z)

    gates = jax.nn.sigmoid(gate_logits).reshape(bsz, t, N_BRANCH, D_MODEL)
    merged = (gates[..., 0, :] * (y_a @ a_proj)
              + gates[..., 1, :] * (y_b @ b_proj)
              + gates[..., 2, :] * (y_c @ c_proj))
    return x + merged @ w_out


def setup_inputs(seed: int = 0) -> dict:
    key = jax.random.key(seed)
    ks = jax.random.split(key, 24)
    f32 = jnp.float32
    L = DEPTH

    def nrm(k, shape, fan_in):
        return jax.random.normal(k, shape, f32) * (fan_in ** -0.5)

    def gain(k, shape):
        return 1.0 + 0.02 * jax.random.normal(k, shape, f32)

    def bias(k, shape):
        return 0.02 * jax.random.normal(k, shape, f32)

    dt = jnp.exp(jax.random.uniform(ks[10], (L, DN_HEADS), f32,
                                    math.log(1e-3), math.log(1e-1)))
    return {
        "x": jax.random.normal(ks[0], (BATCH, SEQ, D_MODEL), f32),
        "norm_g": gain(ks[1], (L, D_MODEL)),
        "w_in": nrm(ks[2], (L, D_MODEL, N_IN_COLS), D_MODEL),
        "a_dw": nrm(ks[3], (L, CONV_K, CONV_WIDTH), CONV_K),
        "a_dw_b": bias(ks[4], (L, CONV_WIDTH)),
        "a_ln_g": gain(ks[5], (L, CONV_WIDTH)),
        "a_ln_b": bias(ks[6], (L, CONV_WIDTH)),
        "a_proj": nrm(ks[7], (L, CONV_WIDTH, D_MODEL), CONV_WIDTH),
        "b_conv": nrm(ks[8], (L, DN_SHORT_K, DN_HEADS * (2 * DN_DK + DN_DV)), DN_SHORT_K),
        "b_a_log": jnp.log(jax.random.uniform(ks[9], (L, DN_HEADS), f32, 1.0, 16.0)),
        "b_dt_bias": dt + jnp.log(-jnp.expm1(-dt)),
        "b_onorm_g": gain(ks[11], (L, DN_DV)),
        "b_proj": nrm(ks[12], (L, DN_HEADS * DN_DV, D_MODEL), DN_HEADS * DN_DV),
        "c_ln_g": gain(ks[13], (L, SG_WIDTH)),
        "c_ln_b": bias(ks[14], (L, SG_WIDTH)),
        "c_ws": nrm(ks[15], (L, SG_GROUPS, SG_CHUNK, SG_CHUNK), SG_CHUNK),
        "c_bs": gain(ks[16], (L, SG_GROUPS, SG_CHUNK)),
        "c_proj": nrm(ks[17], (L, SG_WIDTH, D_MODEL), SG_WIDTH),
        "w_out": nrm(ks[18], (L, D_MODEL, D_MODEL), D_MODEL),
        "final_g": gain(ks[19], (D_MODEL,)),
    }


def reference(x, norm_g, w_in, a_dw, a_dw_b, a_ln_g, a_ln_b, a_proj,
              b_conv, b_a_log, b_dt_bias, b_onorm_g, b_proj,
              c_ln_g, c_ln_b, c_ws, c_bs, c_proj, w_out, final_g):
    for l in range(DEPTH):
        x = _layer(x, norm_g[l], w_in[l], a_dw[l], a_dw_b[l], a_ln_g[l], a_ln_b[l],
                   a_proj[l], b_conv[l], b_a_log[l], b_dt_bias[l], b_onorm_g[l],
                   b_proj[l], c_ln_g[l], c_ln_b[l], c_ws[l], c_bs[l], c_proj[l],
                   w_out[l])
    return _rmsnorm(x, final_g)
```

```python
import functools

import jax
import jax.numpy as jnp
from jax import lax
from jax.experimental import pallas as pl
from jax.experimental.pallas import tpu as pltpu

D_MODEL = 1024
CONV_WIDTH = 512
CONV_K = 31
DN_DK = 128
DN_DV = 128
DN_HEADS = 8
DN_SHORT_K = 4
DN_CHUNK = 64
SG_WIDTH = 512
SG_GROUPS = 4
SG_CHUNK = 128
NORM_EPS = 1e-6

SEQ_TILE = 256
A_HALO = 32
B_HALO = 8
VMEM_LIMIT_BYTES = 56 * 1024 * 1024

_QK = DN_HEADS * DN_DK
_QKV = DN_HEADS * (2 * DN_DK + DN_DV)
OFF_A = 0
OFF_QKV = OFF_A + 3 * CONV_WIDTH
OFF_BZ = OFF_QKV + _QKV
OFF_C = OFF_BZ + DN_HEADS * DN_DV
OFF_G = OFF_C + 3 * SG_WIDTH
N_MAIN = OFF_G + 3 * D_MODEL

BF16 = jnp.bfloat16
F32 = jnp.float32


def _dot(a, b):
    return jnp.dot(a, b, preferred_element_type=F32)


def _dot_nt(a, b):
    return lax.dot_general(a, b, (((1,), (1,)), ((), ())), preferred_element_type=F32)


def _dot_tn(a, b):
    return lax.dot_general(a, b, (((0,), (0,)), ((), ())), preferred_element_type=F32)


def _sigmoid(x):
    return 1.0 / (1.0 + jnp.exp(-x))


def _silu(x):
    return x / (1.0 + jnp.exp(-x))


def _gelu_tanh(x):
    c = 0.7978845608028654
    return 0.5 * x * (1.0 + jnp.tanh(c * (x + 0.044715 * (x * x * x))))


def _softplus(x):
    return jnp.maximum(x, 0.0) + jnp.log1p(jnp.exp(-jnp.abs(x)))


def _layernorm(x, g, b):
    mu = jnp.mean(x, axis=-1, keepdims=True)
    xc = x - mu
    var = jnp.mean(xc * xc, axis=-1, keepdims=True)
    return xc * lax.rsqrt(var + NORM_EPS) * g + b


def _layer_kernel(x_ref, ng_ref, wmain_ref, wba_ref, wbat_ref,
                  adw_ref, adwb_ref, alng_ref, alnb_ref, aproj_ref,
                  bconv_ref, alog_r_ref, dtb_r_ref, alog_c_ref, dtb_c_ref, ong_ref, bproj_ref,
                  clng_ref, clnb_ref, cws_ref, cbst_ref, cproj_ref, wout_ref, fg_ref,
                  out_ref,
                  ahist_s, qhist_s, state_s, o_s, *, final_norm):
    tt = SEQ_TILE

    @pl.when(pl.program_id(1) == 0)
    def _reset():
        ahist_s[0:A_HALO, :] = jnp.zeros((A_HALO, CONV_WIDTH), F32)
        qhist_s[0:B_HALO, :] = jnp.zeros((B_HALO, _QKV), F32)
        state_s[...] = jnp.zeros(state_s.shape, F32)

    x = x_ref[0]
    h = x * lax.rsqrt(jnp.mean(x * x, axis=-1, keepdims=True) + NORM_EPS) * ng_ref[...]
    hb = h.astype(BF16)

    qhist_s[B_HALO:B_HALO + tt, :] = _dot(hb, wmain_ref[:, OFF_QKV:OFF_QKV + _QKV])
    conv = bconv_ref[0:1, :] * qhist_s[B_HALO - 3:B_HALO - 3 + tt, :]
    for j in range(1, DN_SHORT_K):
        conv = conv + bconv_ref[j:j + 1, :] * qhist_s[B_HALO - 3 + j:B_HALO - 3 + j + tt, :]
    qhist_s[0:B_HALO, :] = qhist_s[tt:tt + B_HALO, :]
    qkv = _silu(conv)

    ba = _dot(hb, wba_ref[...])
    bat = _dot_nt(wbat_ref[...], hb)
    beta = _sigmoid(ba[:, 0:DN_HEADS])
    g_col = -jnp.exp(alog_r_ref[...]) * _softplus(ba[:, DN_HEADS:2 * DN_HEADS] + dtb_r_ref[...])
    g_row = -jnp.exp(alog_c_ref[...]) * _softplus(bat[DN_HEADS:2 * DN_HEADS, :] + dtb_c_ref[...])

    ri = lax.broadcasted_iota(jnp.int32, (tt, tt), 0)
    ci = lax.broadcasted_iota(jnp.int32, (tt, tt), 1)
    same = (ri >> 6) == (ci >> 6)
    lo_tri = jnp.where(same & (ci <= ri), 1.0, 0.0).astype(BF16)
    up_tri = jnp.where(same & (ri <= ci), 1.0, 0.0).astype(BF16)
    gch = g_col.astype(BF16)
    gcl = (g_col - gch.astype(F32)).astype(BF16)
    gc_col = _dot(lo_tri, gch) + _dot(lo_tri, gcl)
    grh = g_row.astype(BF16)
    grl = (g_row - grh.astype(F32)).astype(BF16)
    gc_row = _dot(grh, up_tri) + _dot(grl, up_tri)
    eg_col = jnp.exp(gc_col)

    ii = lax.broadcasted_iota(jnp.int32, (DN_CHUNK, DN_CHUNK), 0)
    jj = lax.broadcasted_iota(jnp.int32, (DN_CHUNK, DN_CHUNK), 1)
    incl = ii >= jj
    strict = ii > jj
    eye = jnp.where(ii == jj, 1.0, 0.0).astype(F32)

    q_scale = DN_DK ** -0.5
    for hd in range(DN_HEADS):
        qh = qkv[:, hd * DN_DK:(hd + 1) * DN_DK]
        kh = qkv[:, _QK + hd * DN_DK:_QK + (hd + 1) * DN_DK]
        vh = qkv[:, 2 * _QK + hd * DN_DV:2 * _QK + (hd + 1) * DN_DV]
        qh = qh * (lax.rsqrt(jnp.sum(qh * qh, axis=-1, keepdims=True) + NORM_EPS) * q_scale)
        kh = kh * lax.rsqrt(jnp.sum(kh * kh, axis=-1, keepdims=True) + NORM_EPS)
        bcol_all = beta[:, hd:hd + 1]
        gcol_all = gc_col[:, hd:hd + 1]
        egcol_all = eg_col[:, hd:hd + 1]
        s_mat = state_s[hd]
        for c in range(tt // DN_CHUNK):
            r0, r1 = c * DN_CHUNK, (c + 1) * DN_CHUNK
            qc, kc, vc = qh[r0:r1], kh[r0:r1], vh[r0:r1]
            bcol, gcol, egcol = bcol_all[r0:r1], gcol_all[r0:r1], egcol_all[r0:r1]
            grow = gc_row[hd:hd + 1, r0:r1]
            gam = jnp.exp(jnp.where(incl, gcol - grow, -jnp.inf))
            kcb = kc.astype(BF16)
            qkk = _dot_nt(jnp.concatenate([qc, kc], axis=0).astype(BF16), kcb)
            qk, kk = qkk[0:DN_CHUNK], qkk[DN_CHUNK:]
            neg_l = jnp.where(strict, -(bcol * kk * gam), 0.0)
            t_inv = eye + neg_l
            pw = neg_l
            for _ in range(5):
                pwb = pw.astype(BF16)
                pw = _dot(pwb, pwb)
                t_inv = t_inv + _dot(t_inv.astype(BF16), pw.astype(BF16))
            rhs = jnp.concatenate([vc * bcol, kc * (bcol * egcol)], axis=1).astype(BF16)
            sol = _dot(t_inv.astype(BF16), rhs)
            u, w = sol[:, 0:DN_DV], sol[:, DN_DV:]
            wq = jnp.concatenate([w, qc * egcol], axis=0).astype(BF16)
            ws = _dot(wq, s_mat.astype(BF16))
            v_new = u - ws[0:DN_CHUNK]
            v_new_b = v_new.astype(BF16)
            o_c = ws[DN_CHUNK:] + _dot((qk * gam).astype(BF16), v_new_b)
            o_s[r0:r1, hd * DN_DV:(hd + 1) * DN_DV] = o_c
            g_last = gcol[DN_CHUNK - 1:DN_CHUNK, :]
            k_dec = (kc * jnp.exp(g_last - gcol)).astype(BF16)
            s_mat = jnp.exp(g_last) * s_mat + _dot_tn(k_dec, v_new_b)
        state_s[hd] = s_mat

    z_b = _silu(_dot(hb, wmain_ref[:, OFF_BZ:OFF_BZ + DN_HEADS * DN_DV]))
    for hd in range(DN_HEADS):
        oh = o_s[:, hd * DN_DV:(hd + 1) * DN_DV]
        oh = oh * lax.rsqrt(jnp.mean(oh * oh, axis=-1, keepdims=True) + NORM_EPS) * ong_ref[...]
        o_s[:, hd * DN_DV:(hd + 1) * DN_DV] = oh
    y_b = (o_s[...] * z_b).astype(BF16)
    gate_b = _sigmoid(_dot(hb, wmain_ref[:, OFF_G + D_MODEL:OFF_G + 2 * D_MODEL]))
    merged = gate_b * _dot(y_b, bproj_ref[...])

    pa = _dot(hb, wmain_ref[:, OFF_A:OFF_A + 3 * CONV_WIDTH])
    ahist_s[A_HALO:A_HALO + tt, :] = pa[:, 0:CONV_WIDTH] * _sigmoid(pa[:, CONV_WIDTH:2 * CONV_WIDTH])
    base = A_HALO - (CONV_K - 1)
    acc = adw_ref[0:1, :] * ahist_s[base:base + tt, :]
    for j in range(1, CONV_K):
        acc = acc + adw_ref[j:j + 1, :] * ahist_s[base + j:base + j + tt, :]
    ahist_s[0:A_HALO, :] = ahist_s[tt:tt + A_HALO, :]
    a = _layernorm(acc + adwb_ref[...], alng_ref[...], alnb_ref[...])
    y_a = (_silu(a) * _silu(pa[:, 2 * CONV_WIDTH:])).astype(BF16)
    gate_a = _sigmoid(_dot(hb, wmain_ref[:, OFF_G:OFF_G + D_MODEL]))
    merged = merged + gate_a * _dot(y_a, aproj_ref[...])

    pc = _dot(hb, wmain_ref[:, OFF_C:OFF_C + 3 * SG_WIDTH])
    u_c = _gelu_tanh(pc[:, 0:SG_WIDTH])
    v_c = _layernorm(_gelu_tanh(pc[:, SG_WIDTH:2 * SG_WIDTH]), clng_ref[...], clnb_ref[...])
    v_cb = v_c.astype(BF16)
    si = lax.broadcasted_iota(jnp.int32, (SG_CHUNK, SG_CHUNK), 0)
    sj = lax.broadcasted_iota(jnp.int32, (SG_CHUNK, SG_CHUNK), 1)
    gw = SG_WIDTH // SG_GROUPS
    group_cols = []
    for gidx in range(SG_GROUPS):
        w_tri = jnp.where(si >= sj, cws_ref[gidx], 0.0).astype(BF16)
        bias = cbst_ref[:, gidx:gidx + 1]
        rows = []
        for n in range(tt // SG_CHUNK):
            blk = v_cb[n * SG_CHUNK:(n + 1) * SG_CHUNK, gidx * gw:(gidx + 1) * gw]
            rows.append(_dot(w_tri, blk) + bias)
        group_cols.append(jnp.concatenate(rows, axis=0))
    mixed = jnp.concatenate(group_cols, axis=1)
    y_c = (u_c * mixed * _silu(pc[:, 2 * SG_WIDTH:])).astype(BF16)
    gate_c = _sigmoid(_dot(hb, wmain_ref[:, OFF_G + 2 * D_MODEL:OFF_G + 3 * D_MODEL]))
    merged = merged + gate_c * _dot(y_c, cproj_ref[...])

    y = x + _dot(merged.astype(BF16), wout_ref[...])
    if final_norm:
        y = y * lax.rsqrt(jnp.mean(y * y, axis=-1, keepdims=True) + NORM_EPS) * fg_ref[...]
    out_ref[0] = y


def _const_spec(shape):
    nd = len(shape)
    return pl.BlockSpec(shape, lambda b, t: (0,) * nd, pipeline_mode=pl.Buffered(1))


def _layer_call(x, params, final_g, final_norm):
    bsz, seq, d = x.shape
    assert d == D_MODEL and seq % SEQ_TILE == 0
    consts = list(params) + [final_g]
    tile_spec = pl.BlockSpec((1, SEQ_TILE, D_MODEL), lambda b, t: (b, t, 0))
    return pl.pallas_call(
        functools.partial(_layer_kernel, final_norm=final_norm),
        grid=(bsz, seq // SEQ_TILE),
        in_specs=[tile_spec] + [_const_spec(c.shape) for c in consts],
        out_specs=tile_spec,
        out_shape=jax.ShapeDtypeStruct(x.shape, F32),
        scratch_shapes=[
            pltpu.VMEM((SEQ_TILE + A_HALO, CONV_WIDTH), F32),
            pltpu.VMEM((SEQ_TILE + B_HALO, _QKV), F32),
            pltpu.VMEM((DN_HEADS, DN_DK, DN_DV), F32),
            pltpu.VMEM((SEQ_TILE, DN_HEADS * DN_DV), F32),
        ],
        compiler_params=pltpu.CompilerParams(
            dimension_semantics=("arbitrary", "arbitrary"),
            vmem_limit_bytes=VMEM_LIMIT_BYTES),
    )(x, *consts)


def _prep_layer(norm_g, w_in, a_dw, a_dw_b, a_ln_g, a_ln_b, a_proj, b_conv, b_a_log, b_dt_bias,
                b_onorm_g, b_proj, c_ln_g, c_ln_b, c_ws, c_bs, c_proj, w_out):
    o = 0
    pieces = {}
    for name, n in (("a", 3 * CONV_WIDTH), ("qkv", _QKV), ("bz", DN_HEADS * DN_DV),
                    ("ba", 2 * DN_HEADS), ("c", 3 * SG_WIDTH), ("g", 3 * D_MODEL)):
        pieces[name] = w_in[:, o:o + n]
        o += n
    w_main = jnp.concatenate([pieces["a"], pieces["qkv"], pieces["bz"], pieces["c"], pieces["g"]],
                             axis=1).astype(BF16)
    w_ba = pieces["ba"].astype(BF16)
    row = lambda v: v.reshape(1, -1)
    col = lambda v: v.reshape(-1, 1)
    return (row(norm_g), w_main, w_ba, w_ba.T,
            a_dw, row(a_dw_b), row(a_ln_g), row(a_ln_b), a_proj.astype(BF16),
            b_conv, row(b_a_log), row(b_dt_bias), col(b_a_log), col(b_dt_bias), row(b_onorm_g),
            b_proj.astype(BF16),
            row(c_ln_g), row(c_ln_b), c_ws, c_bs.T, c_proj.astype(BF16), w_out.astype(BF16))


def kernel(x, norm_g, w_in, a_dw, a_dw_b, a_ln_g, a_ln_b, a_proj, b_conv, b_a_log, b_dt_bias,
           b_onorm_g, b_proj, c_ln_g, c_ln_b, c_ws, c_bs, c_proj, w_out, final_g):
    depth = w_in.shape[0]
    fg = final_g.reshape(1, -1)
    for l in range(depth):
        params = _prep_layer(norm_g[l], w_in[l], a_dw[l], a_dw_b[l], a_ln_g[l], a_ln_b[l], a_proj[l],
                             b_conv[l], b_a_log[l], b_dt_bias[l], b_onorm_g[l], b_proj[l],
                             c_ln_g[l], c_ln_b[l], c_ws[l], c_bs[l], c_proj[l], w_out[l])
        x = _layer_call(x, params, fg, final_norm=(l == depth - 1))
    return x
```

```python
import functools

import jax
import jax.numpy as jnp
from jax import lax
from jax.experimental import pallas as pl
from jax.experimental.pallas import tpu as pltpu

D_MODEL = 1024
CONV_WIDTH = 512
CONV_K = 31
DN_DK = 128
DN_DV = 128
DN_HEADS = 8
DN_SHORT_K = 4
DN_CHUNK = 64
SG_WIDTH = 512
SG_GROUPS = 4
SG_CHUNK = 128
NORM_EPS = 1e-6

SEQ_TILE = 256
N_CHUNKS = SEQ_TILE // DN_CHUNK
A_HALO = 32
B_HALO = 8
SUBLANES = 8
HEAD_GROUP = 4
N_GROUPS = DN_HEADS // HEAD_GROUP
PACK_W = HEAD_GROUP * DN_CHUNK
SLAB = 512
VMEM_LIMIT_BYTES = 56 * 1024 * 1024

_QK = DN_HEADS * DN_DK
_QKV = DN_HEADS * (2 * DN_DK + DN_DV)
OFF_A = 0
OFF_QKV = OFF_A + 3 * CONV_WIDTH
OFF_C = OFF_QKV + _QKV
OFF_BZ = OFF_C + 3 * SG_WIDTH
OFF_G = OFF_BZ + DN_HEADS * DN_DV
N_MAIN = OFF_G + 3 * D_MODEL
N_LATE = (N_MAIN - OFF_BZ) // SLAB

BF16 = jnp.bfloat16
F32 = jnp.float32


def _dot(a, b):
    return jnp.dot(a, b, preferred_element_type=F32)


def _dot_nt(a, b):
    return lax.dot_general(a, b, (((1,), (1,)), ((), ())), preferred_element_type=F32)


def _dot_tn(a, b):
    return lax.dot_general(a, b, (((0,), (0,)), ((), ())), preferred_element_type=F32)


def _sigmoid(x):
    return 1.0 / (1.0 + jnp.exp(-x))


def _silu(x):
    return x / (1.0 + jnp.exp(-x))


def _gelu_tanh(x):
    c = 0.7978845608028654
    return 0.5 * x * (1.0 + jnp.tanh(c * (x + 0.044715 * (x * x * x))))


def _softplus(x):
    return jnp.maximum(x, 0.0) + jnp.log1p(jnp.exp(-jnp.abs(x)))


def _layernorm(x, g, b):
    mu = jnp.mean(x, axis=-1, keepdims=True)
    xc = x - mu
    var = jnp.mean(xc * xc, axis=-1, keepdims=True)
    return xc * lax.rsqrt(var + NORM_EPS) * g + b


def _block_diag(x, n, mask):
    return jnp.where(mask, jnp.concatenate([x] * n, axis=0), jnp.zeros((), x.dtype))


def _pair_diag(a, b):
    z = jnp.zeros(a.shape, a.dtype)
    return jnp.concatenate([jnp.concatenate([a, z], axis=1), jnp.concatenate([z, b], axis=1)], axis=0)


def _layer_kernel(x_ref, ng_ref, wmain_ref, wba_ref,
                  adw_ref, adwb_ref, alng_ref, alnb_ref, aproj_ref,
                  bconv_ref, alog_ref, dtb_ref, ong_ref, bproj_ref,
                  clng_ref, clnb_ref, cws_ref, cbst_ref, cproj_ref, wout_ref, fg_ref,
                  out_ref,
                  ahist_s, qhist_s, state_s, o_s, *, final_norm):
    tt = SEQ_TILE
    ck = DN_CHUNK

    @pl.when(pl.program_id(1) == 0)
    def _reset():
        ahist_s[0:A_HALO, :] = jnp.zeros((A_HALO, CONV_WIDTH), F32)
        qhist_s[0:B_HALO, :] = jnp.zeros((B_HALO, _QKV), F32)
        state_s[...] = jnp.zeros(state_s.shape, F32)

    x = x_ref[0]
    h = x * lax.rsqrt(jnp.mean(x * x, axis=-1, keepdims=True) + NORM_EPS) * ng_ref[...]
    hb = h.astype(BF16)

    qhist_s[B_HALO:B_HALO + tt, :] = _dot(hb, wmain_ref[:, OFF_QKV:OFF_QKV + _QKV])
    ba = _dot(hb, wba_ref[...])
    pa = _dot(hb, wmain_ref[:, OFF_A:OFF_A + 3 * CONV_WIDTH])
    pc = _dot(hb, wmain_ref[:, OFF_C:OFF_C + 3 * SG_WIDTH])

    conv = bconv_ref[0:1, :] * qhist_s[B_HALO - 3:B_HALO - 3 + tt, :]
    for j in range(1, DN_SHORT_K):
        conv = conv + bconv_ref[j:j + 1, :] * qhist_s[B_HALO - 3 + j:B_HALO - 3 + j + tt, :]
    qhist_s[0:B_HALO, :] = qhist_s[tt:tt + B_HALO, :]
    qkv = _silu(conv)

    beta = _sigmoid(ba[:, 0:DN_HEADS])
    g_col = -jnp.exp(alog_ref[...]) * _softplus(ba[:, DN_HEADS:2 * DN_HEADS] + dtb_ref[...])

    ri = lax.broadcasted_iota(jnp.int32, (tt, tt), 0)
    ci = lax.broadcasted_iota(jnp.int32, (tt, tt), 1)
    lo_tri = jnp.where(((ri >> 6) == (ci >> 6)) & (ci <= ri), 1.0, 0.0).astype(BF16)
    gch = g_col.astype(BF16)
    gcl = (g_col - gch.astype(F32)).astype(BF16)
    gc = _dot(lo_tri, gch) + _dot(lo_tri, gcl)
    g_last = jnp.concatenate(
        [jnp.broadcast_to(gc[(c + 1) * ck - 1:(c + 1) * ck, :], (ck, DN_HEADS)) for c in range(N_CHUNKS)],
        axis=0)
    eg = jnp.exp(gc)
    beta_eg = beta * eg
    dec = jnp.exp(g_last - gc)
    d_last = jnp.exp(g_last)

    q_scale = DN_DK ** -0.5
    qn, kn, vv = [], [], []
    for hd in range(DN_HEADS):
        qh = qkv[:, hd * DN_DK:(hd + 1) * DN_DK]
        kh = qkv[:, _QK + hd * DN_DK:_QK + (hd + 1) * DN_DK]
        qn.append(qh * (lax.rsqrt(jnp.sum(qh * qh, axis=-1, keepdims=True) + NORM_EPS) * q_scale))
        kn.append(kh * lax.rsqrt(jnp.sum(kh * kh, axis=-1, keepdims=True) + NORM_EPS))
        vv.append(qkv[:, 2 * _QK + hd * DN_DV:2 * _QK + (hd + 1) * DN_DV])

    pi = lax.broadcasted_iota(jnp.int32, (ck, PACK_W), 0)
    pl_ = lax.broadcasted_iota(jnp.int32, (ck, PACK_W), 1)
    pj = pl_ & (ck - 1)
    ph = pl_ >> 6
    incl_pk = pi >= pj
    strict_pk = pi > pj
    eye_pk = pi == pj
    eye_f = jnp.where(eye_pk, 1.0, 0.0).astype(F32)
    bd_i = lax.broadcasted_iota(jnp.int32, (PACK_W, PACK_W), 0)
    bd_j = lax.broadcasted_iota(jnp.int32, (PACK_W, PACK_W), 1)
    bd_mask = (bd_i >> 6) == (bd_j >> 6)
    kd_i = lax.broadcasted_iota(jnp.int32, (PACK_W, HEAD_GROUP * DN_DK), 0)
    kd_j = lax.broadcasted_iota(jnp.int32, (PACK_W, HEAD_GROUP * DN_DK), 1)
    bdk_mask = (kd_i >> 6) == (kd_j >> 7)

    def pack_cols(arr, g, r0, r1):
        cols = [arr[r0:r1, g * HEAD_GROUP + i:g * HEAD_GROUP + i + 1] for i in range(HEAD_GROUP)]
        return jnp.where(ph == 0, cols[0], jnp.where(ph == 1, cols[1], jnp.where(ph == 2, cols[2], cols[3])))

    items = [(g, c) for c in range(N_CHUNKS) for g in range(N_GROUPS)]

    qk_pk, neg_l = {}, {}
    for (g, c) in items:
        r0, r1 = c * ck, (c + 1) * ck
        q4 = jnp.concatenate([qn[g * HEAD_GROUP + i][r0:r1] for i in range(HEAD_GROUP)], axis=1)
        k4 = jnp.concatenate([kn[g * HEAD_GROUP + i][r0:r1] for i in range(HEAD_GROUP)], axis=1)
        k4b = k4.astype(BF16)
        qkk = _dot_nt(jnp.concatenate([q4.astype(BF16), k4b], axis=0),
                      _block_diag(k4b, HEAD_GROUP, bdk_mask))
        gcol = pack_cols(gc, g, r0, r1)
        grow = jnp.sum(jnp.where(eye_pk, gcol, 0.0), axis=0, keepdims=True)
        gam = jnp.exp(jnp.where(incl_pk, gcol - grow, -jnp.inf))
        qk_pk[(g, c)] = (qkk[0:ck] * gam).astype(BF16)
        neg_l[(g, c)] = jnp.where(strict_pk, -(pack_cols(beta, g, r0, r1) * qkk[ck:] * gam), 0.0)

    t_inv = {it: eye_f + neg_l[it] for it in items}
    pw = {}
    for it in items:
        pb = neg_l[it].astype(BF16)
        pw[it] = _dot(pb, _block_diag(pb, HEAD_GROUP, bd_mask))
    for step in range(5):
        for it in items:
            pb = pw[it].astype(BF16)
            bd = _block_diag(pb, HEAD_GROUP, bd_mask)
            if step < 4:
                res = _dot(jnp.concatenate([t_inv[it].astype(BF16), pb], axis=0), bd)
                t_inv[it] = t_inv[it] + res[0:ck]
                pw[it] = res[ck:]
            else:
                t_inv[it] = t_inv[it] + _dot(t_inv[it].astype(BF16), bd)

    u_all, w_all = {}, {}
    zero_rhs = jnp.zeros((ck, DN_DV + DN_DK), BF16)
    for (g, c) in items:
        r0, r1 = c * ck, (c + 1) * ck
        for i in range(HEAD_GROUP):
            hd = g * HEAD_GROUP + i
            rhs = jnp.concatenate([vv[hd][r0:r1] * beta[r0:r1, hd:hd + 1],
                                   kn[hd][r0:r1] * beta_eg[r0:r1, hd:hd + 1]], axis=1).astype(BF16)
            rhs2 = jnp.concatenate([rhs, zero_rhs] if i % 2 == 0 else [zero_rhs, rhs], axis=0)
            t_sl = t_inv[(g, c)][:, (i // 2) * 2 * ck:(i // 2 + 1) * 2 * ck].astype(BF16)
            sol = _dot(t_sl, rhs2)
            u_all[(c, hd)] = sol[:, 0:DN_DV]
            w_all[(c, hd)] = sol[:, DN_DV:]

    late = []

    def issue_slab():
        n = len(late)
        if n < N_LATE:
            late.append(_dot(hb, wmain_ref[:, OFF_BZ + n * SLAB:OFF_BZ + (n + 1) * SLAB]))

    n_pairs = DN_HEADS // 2
    s_mat = [state_s[hd] for hd in range(DN_HEADS)]
    for c in range(N_CHUNKS):
        r0, r1 = c * ck, (c + 1) * ck
        ws = []
        for p in range(n_pairs):
            wq = []
            for hd in (2 * p, 2 * p + 1):
                wq.append(jnp.concatenate([w_all[(c, hd)], qn[hd][r0:r1] * eg[r0:r1, hd:hd + 1]], axis=0))
            ws.append(_dot(jnp.concatenate(wq, axis=1).astype(BF16),
                           _pair_diag(s_mat[2 * p].astype(BF16), s_mat[2 * p + 1].astype(BF16))))
        issue_slab()
        for p in range(n_pairs):
            h0, h1 = 2 * p, 2 * p + 1
            u2 = jnp.concatenate([u_all[(c, h0)], u_all[(c, h1)]], axis=1)
            v_new = (u2 - ws[p][0:ck]).astype(BF16)
            bdv = _pair_diag(v_new[:, 0:DN_DV], v_new[:, DN_DV:])
            a_sl = qk_pk[(p // 2, c)][:, (p % 2) * 2 * ck:(p % 2 + 1) * 2 * ck]
            o_s[r0:r1, h0 * DN_DV:(h1 + 1) * DN_DV] = ws[p][ck:] + _dot(a_sl, bdv)
            kd = jnp.concatenate([(kn[hd][r0:r1] * dec[r0:r1, hd:hd + 1]).astype(BF16) for hd in (h0, h1)],
                                 axis=0)
            upd = _dot_tn(kd, bdv)
            s_mat[h0] = d_last[r0:r0 + 1, h0:h0 + 1] * s_mat[h0] + upd[:, 0:DN_DV]
            s_mat[h1] = d_last[r0:r0 + 1, h1:h1 + 1] * s_mat[h1] + upd[:, DN_DV:]
        issue_slab()
    for hd in range(DN_HEADS):
        state_s[hd] = s_mat[hd]
    while len(late) < N_LATE:
        issue_slab()
    late = jnp.concatenate(late, axis=1)
    z_b = _silu(late[:, 0:DN_HEADS * DN_DV])
    gates = late[:, DN_HEADS * DN_DV:]

    for hd in range(DN_HEADS):
        oh = o_s[:, hd * DN_DV:(hd + 1) * DN_DV]
        oh = oh * lax.rsqrt(jnp.mean(oh * oh, axis=-1, keepdims=True) + NORM_EPS) * ong_ref[...]
        o_s[:, hd * DN_DV:(hd + 1) * DN_DV] = oh
    y_b = (o_s[...] * z_b).astype(BF16)
    merged = _sigmoid(gates[:, D_MODEL:2 * D_MODEL]) * _dot(y_b, bproj_ref[...])

    ahist_s[A_HALO:A_HALO + tt, :] = pa[:, 0:CONV_WIDTH] * _sigmoid(pa[:, CONV_WIDTH:2 * CONV_WIDTH])
    win0 = A_HALO - SUBLANES
    acc = None
    for r in range(SUBLANES):
        part = None
        for m in range((CONV_K - 1 - r) // SUBLANES + 1):
            tap = CONV_K - 1 - (SUBLANES * m + r)
            lo = win0 - SUBLANES * m
            term = adw_ref[tap:tap + 1, :] * ahist_s[lo:lo + tt + SUBLANES, :]
            part = term if part is None else part + term
        part = part[SUBLANES - r:SUBLANES - r + tt]
        acc = part if acc is None else acc + part
    ahist_s[0:A_HALO, :] = ahist_s[tt:tt + A_HALO, :]
    a = _layernorm(acc + adwb_ref[...], alng_ref[...], alnb_ref[...])
    y_a = (_silu(a) * _silu(pa[:, 2 * CONV_WIDTH:])).astype(BF16)
    merged = merged + _sigmoid(gates[:, 0:D_MODEL]) * _dot(y_a, aproj_ref[...])

    u_c = _gelu_tanh(pc[:, 0:SG_WIDTH])
    v_c = _layernorm(_gelu_tanh(pc[:, SG_WIDTH:2 * SG_WIDTH]), clng_ref[...], clnb_ref[...])
    v_cb = v_c.astype(BF16)
    si = lax.broadcasted_iota(jnp.int32, (SG_CHUNK, SG_CHUNK), 0)
    sj = lax.broadcasted_iota(jnp.int32, (SG_CHUNK, SG_CHUNK), 1)
    gw = SG_WIDTH // SG_GROUPS
    group_cols = []
    for gidx in range(SG_GROUPS):
        w_tri = jnp.where(si >= sj, cws_ref[gidx], 0.0).astype(BF16)
        bias = cbst_ref[:, gidx:gidx + 1]
        rows = []
        for n in range(tt // SG_CHUNK):
            blk = v_cb[n * SG_CHUNK:(n + 1) * SG_CHUNK, gidx * gw:(gidx + 1) * gw]
            rows.append(_dot(w_tri, blk) + bias)
        group_cols.append(jnp.concatenate(rows, axis=0))
    mixed = jnp.concatenate(group_cols, axis=1)
    y_c = (u_c * mixed * _silu(pc[:, 2 * SG_WIDTH:])).astype(BF16)
    merged = merged + _sigmoid(gates[:, 2 * D_MODEL:]) * _dot(y_c, cproj_ref[...])

    y = x + _dot(merged.astype(BF16), wout_ref[...])
    if final_norm:
        y = y * lax.rsqrt(jnp.mean(y * y, axis=-1, keepdims=True) + NORM_EPS) * fg_ref[...]
    out_ref[0] = y


def _const_spec(shape):
    nd = len(shape)
    return pl.BlockSpec(shape, lambda b, t: (0,) * nd, pipeline_mode=pl.Buffered(1))


def _layer_call(x, params, final_g, final_norm):
    bsz, seq, d = x.shape
    assert d == D_MODEL and seq % SEQ_TILE == 0
    consts = list(params) + [final_g]
    tile_spec = pl.BlockSpec((1, SEQ_TILE, D_MODEL), lambda b, t: (b, t, 0))
    return pl.pallas_call(
        functools.partial(_layer_kernel, final_norm=final_norm),
        grid=(bsz, seq // SEQ_TILE),
        in_specs=[tile_spec] + [_const_spec(c.shape) for c in consts],
        out_specs=tile_spec,
        out_shape=jax.ShapeDtypeStruct(x.shape, F32),
        scratch_shapes=[
            pltpu.VMEM((SEQ_TILE + A_HALO, CONV_WIDTH), F32),
            pltpu.VMEM((SEQ_TILE + B_HALO, _QKV), F32),
            pltpu.VMEM((DN_HEADS, DN_DK, DN_DV), F32),
            pltpu.VMEM((SEQ_TILE, DN_HEADS * DN_DV), F32),
        ],
        compiler_params=pltpu.CompilerParams(
            dimension_semantics=("arbitrary", "arbitrary"),
            vmem_limit_bytes=VMEM_LIMIT_BYTES),
    )(x, *consts)


def _prep_layer(norm_g, w_in, a_dw, a_dw_b, a_ln_g, a_ln_b, a_proj, b_conv, b_a_log, b_dt_bias,
                b_onorm_g, b_proj, c_ln_g, c_ln_b, c_ws, c_bs, c_proj, w_out):
    o = 0
    pieces = {}
    for name, n in (("a", 3 * CONV_WIDTH), ("qkv", _QKV), ("bz", DN_HEADS * DN_DV),
                    ("ba", 2 * DN_HEADS), ("c", 3 * SG_WIDTH), ("g", 3 * D_MODEL)):
        pieces[name] = w_in[:, o:o + n]
        o += n
    w_main = jnp.concatenate([pieces["a"], pieces["qkv"], pieces["c"], pieces["bz"], pieces["g"]],
                             axis=1).astype(BF16)
    row = lambda v: v.reshape(1, -1)
    return (row(norm_g), w_main, pieces["ba"].astype(BF16),
            a_dw, row(a_dw_b), row(a_ln_g), row(a_ln_b), a_proj.astype(BF16),
            b_conv, row(b_a_log), row(b_dt_bias), row(b_onorm_g), b_proj.astype(BF16),
            row(c_ln_g), row(c_ln_b), c_ws, c_bs.T, c_proj.astype(BF16), w_out.astype(BF16))


def kernel(x, norm_g, w_in, a_dw, a_dw_b, a_ln_g, a_ln_b, a_proj, b_conv, b_a_log, b_dt_bias,
           b_onorm_g, b_proj, c_ln_g, c_ln_b, c_ws, c_bs, c_proj, w_out, final_g):
    depth = w_in.shape[0]
    fg = final_g.reshape(1, -1)
    for l in range(depth):
        params = _prep_layer(norm_g[l], w_in[l], a_dw[l], a_dw_b[l], a_ln_g[l], a_ln_b[l], a_proj[l],
                             b_conv[l], b_a_log[l], b_dt_bias[l], b_onorm_g[l], b_proj[l],
                             c_ln_g[l], c_ln_b[l], c_ws[l], c_bs[l], c_proj[l], w_out[l])
        x = _layer_call(x, params, fg, final_norm=(l == depth - 1))
    return x
```

```python
import functools

import jax
import jax.numpy as jnp
from jax import lax
from jax.experimental import pallas as pl
from jax.experimental.pallas import tpu as pltpu

D_MODEL = 1024
CONV_WIDTH = 512
CONV_K = 31
DN_DK = 128
DN_DV = 128
DN_HEADS = 8
DN_SHORT_K = 4
DN_CHUNK = 64
SG_WIDTH = 512
SG_GROUPS = 4
SG_CHUNK = 128
NORM_EPS = 1e-6

SEQ_TILE = 256
N_CHUNKS = SEQ_TILE // DN_CHUNK
A_HALO = 32
B_HALO = 8
SUBLANES = 8
LANES = 128
HEAD_GROUP = 4
N_GROUPS = DN_HEADS // HEAD_GROUP
N_PAIRS = DN_HEADS // 2
PACK_W = HEAD_GROUP * DN_CHUNK
SLAB = 512
VMEM_LIMIT_BYTES = 56 * 1024 * 1024

_QK = DN_HEADS * DN_DK
_QKV = DN_HEADS * (2 * DN_DK + DN_DV)
OFF_A = 0
OFF_QKV = OFF_A + 3 * CONV_WIDTH
OFF_C = OFF_QKV + _QKV
OFF_BZ = OFF_C + 3 * SG_WIDTH
OFF_G = OFF_BZ + DN_HEADS * DN_DV
N_MAIN = OFF_G + 3 * D_MODEL
N_C_SLABS = (OFF_BZ - OFF_C) // SLAB
N_SLABS = (N_MAIN - OFF_C) // SLAB

BF16 = jnp.bfloat16
F32 = jnp.float32


def _dot(a, b):
    return jnp.dot(a, b, preferred_element_type=F32)


def _dot_nt(a, b):
    return lax.dot_general(a, b, (((1,), (1,)), ((), ())), preferred_element_type=F32)


def _dot_tn(a, b):
    return lax.dot_general(a, b, (((0,), (0,)), ((), ())), preferred_element_type=F32)


def _sigmoid(x):
    return 1.0 / (1.0 + jnp.exp(-x))


def _silu_of_half(hx):
    return hx + hx * jnp.tanh(hx)


def _gate2_of_half(hx):
    return 1.0 + jnp.tanh(hx)


def _gelu_tanh(x):
    c = 0.7978845608028654
    return 0.5 * x * (1.0 + jnp.tanh(c * (x + 0.044715 * (x * x * x))))


def _softplus(x):
    return jnp.maximum(x, 0.0) + jnp.log1p(jnp.exp(-jnp.abs(x)))


def _layernorm(x, g, b):
    mu = jnp.mean(x, axis=-1, keepdims=True)
    xc = x - mu
    var = jnp.mean(xc * xc, axis=-1, keepdims=True)
    return xc * lax.rsqrt(var + NORM_EPS) * g + b


def _pair_diag(a, b):
    z = jnp.zeros(a.shape, a.dtype)
    return jnp.concatenate([jnp.concatenate([a, z], axis=1), jnp.concatenate([z, b], axis=1)], axis=0)


def _block_diag4(x, lane_lo):
    zero = jnp.zeros((), x.dtype)
    z = jnp.zeros((x.shape[0], LANES), x.dtype)
    left, right = x[:, 0:LANES], x[:, LANES:]
    return jnp.concatenate([
        jnp.concatenate([jnp.where(lane_lo, left, zero), z], axis=1),
        jnp.concatenate([jnp.where(lane_lo, zero, left), z], axis=1),
        jnp.concatenate([z, jnp.where(lane_lo, right, zero)], axis=1),
        jnp.concatenate([z, jnp.where(lane_lo, zero, right)], axis=1)], axis=0)


def _mixer_a(pa, ahist_s, adw_ref, adwb_ref, alng_ref, alnb_ref, aproj_ref):
    tt = SEQ_TILE
    ahist_s[A_HALO:A_HALO + tt, :] = pa[:, 0:CONV_WIDTH] * _gate2_of_half(pa[:, CONV_WIDTH:2 * CONV_WIDTH])
    win0 = A_HALO - SUBLANES
    acc = None
    for r in range(SUBLANES):
        part = None
        for m in range((CONV_K - 1 - r) // SUBLANES + 1):
            tap = CONV_K - 1 - (SUBLANES * m + r)
            lo = win0 - SUBLANES * m
            term = adw_ref[tap:tap + 1, :] * ahist_s[lo:lo + tt + SUBLANES, :]
            part = term if part is None else part + term
        part = part[SUBLANES - r:SUBLANES - r + tt]
        acc = part if acc is None else acc + part
    ahist_s[0:A_HALO, :] = ahist_s[tt:tt + A_HALO, :]
    a = _layernorm(acc + adwb_ref[...], alng_ref[...], alnb_ref[...])
    y_a = (_silu_of_half(a) * _silu_of_half(pa[:, 2 * CONV_WIDTH:])).astype(BF16)
    return _dot(y_a, aproj_ref[...])


def _mixer_c(pc, clng_ref, clnb_ref, cws_ref, cbst_ref, cproj_ref):
    tt = SEQ_TILE
    u_c = _gelu_tanh(pc[:, 0:SG_WIDTH])
    v_c = _layernorm(_gelu_tanh(pc[:, SG_WIDTH:2 * SG_WIDTH]), clng_ref[...], clnb_ref[...])
    v_cb = v_c.astype(BF16)
    z_c = _silu_of_half(pc[:, 2 * SG_WIDTH:])
    si = lax.broadcasted_iota(jnp.int32, (SG_CHUNK, SG_CHUNK), 0)
    sj = lax.broadcasted_iota(jnp.int32, (SG_CHUNK, SG_CHUNK), 1)
    gw = SG_WIDTH // SG_GROUPS
    group_cols = []
    for gidx in range(SG_GROUPS):
        w_tri = jnp.where(si >= sj, cws_ref[gidx], 0.0).astype(BF16)
        bias = cbst_ref[:, gidx:gidx + 1]
        rows = []
        for n in range(tt // SG_CHUNK):
            blk = v_cb[n * SG_CHUNK:(n + 1) * SG_CHUNK, gidx * gw:(gidx + 1) * gw]
            rows.append(_dot(w_tri, blk) + bias)
        group_cols.append(jnp.concatenate(rows, axis=0))
    mixed = jnp.concatenate(group_cols, axis=1)
    y_c = (u_c * mixed * z_c).astype(BF16)
    return _dot(y_c, cproj_ref[...])


def _layer_kernel(x_ref, ng_ref, wmain_ref, wba_ref,
                  adw_ref, adwb_ref, alng_ref, alnb_ref, aproj_ref,
                  bconv_ref, alog_ref, dtb_ref, ong_ref, bproj_ref,
                  clng_ref, clnb_ref, cws_ref, cbst_ref, cproj_ref, wout_ref, fg_ref,
                  out_ref,
                  ahist_s, qhist_s, state_s, o_s, *, final_norm):
    tt = SEQ_TILE
    ck = DN_CHUNK

    @pl.when(pl.program_id(1) == 0)
    def _reset():
        ahist_s[0:A_HALO, :] = jnp.zeros((A_HALO, CONV_WIDTH), F32)
        qhist_s[0:B_HALO, :] = jnp.zeros((B_HALO, _QKV), F32)
        state_s[...] = jnp.zeros(state_s.shape, F32)

    x = x_ref[0]
    h = x * lax.rsqrt(jnp.mean(x * x, axis=-1, keepdims=True) + NORM_EPS) * ng_ref[...]
    hb = h.astype(BF16)

    qhist_s[B_HALO:B_HALO + tt, :] = _dot(hb, wmain_ref[:, OFF_QKV:OFF_QKV + _QKV])
    ba = _dot(hb, wba_ref[...])
    pa = _dot(hb, wmain_ref[:, OFF_A:OFF_A + 3 * CONV_WIDTH])
    slabs = []

    def issue_slab(limit=N_SLABS):
        n = len(slabs)
        if n < limit:
            slabs.append(_dot(hb, wmain_ref[:, OFF_C + n * SLAB:OFF_C + (n + 1) * SLAB]))

    for _ in range(N_C_SLABS):
        issue_slab()

    conv = bconv_ref[0:1, :] * qhist_s[B_HALO - 3:B_HALO - 3 + tt, :]
    for j in range(1, DN_SHORT_K):
        conv = conv + bconv_ref[j:j + 1, :] * qhist_s[B_HALO - 3 + j:B_HALO - 3 + j + tt, :]
    qhist_s[0:B_HALO, :] = qhist_s[tt:tt + B_HALO, :]
    qkv = _silu_of_half(conv)

    beta = _sigmoid(ba[:, 0:DN_HEADS])
    g_col = -jnp.exp(alog_ref[...]) * _softplus(ba[:, DN_HEADS:2 * DN_HEADS] + dtb_ref[...])

    ri = lax.broadcasted_iota(jnp.int32, (tt, tt), 0)
    ci = lax.broadcasted_iota(jnp.int32, (tt, tt), 1)
    lo_tri = jnp.where(((ri >> 6) == (ci >> 6)) & (ci <= ri), 1.0, 0.0).astype(BF16)
    gch = g_col.astype(BF16)
    gcl = (g_col - gch.astype(F32)).astype(BF16)
    gc = _dot(lo_tri, gch) + _dot(lo_tri, gcl)

    q_scale = DN_DK ** -0.5
    qb, kb, vb = [], [], []
    for hd in range(DN_HEADS):
        qh = qkv[:, hd * DN_DK:(hd + 1) * DN_DK]
        kh = qkv[:, _QK + hd * DN_DK:_QK + (hd + 1) * DN_DK]
        qb.append((qh * (lax.rsqrt(jnp.sum(qh * qh, axis=-1, keepdims=True) + NORM_EPS) * q_scale)).astype(BF16))
        kb.append((kh * lax.rsqrt(jnp.sum(kh * kh, axis=-1, keepdims=True) + NORM_EPS)).astype(BF16))
        vb.append(qkv[:, 2 * _QK + hd * DN_DV:2 * _QK + (hd + 1) * DN_DV].astype(BF16))

    pi = lax.broadcasted_iota(jnp.int32, (ck, PACK_W), 0)
    pl_ = lax.broadcasted_iota(jnp.int32, (ck, PACK_W), 1)
    pj = pl_ & (ck - 1)
    ph = pl_ >> 6
    incl_pk = pi >= pj
    strict_pk = pi > pj
    eye_pk = pi == pj
    eye_f = jnp.where(eye_pk, 1.0, 0.0).astype(F32)
    lane_lo = lax.broadcasted_iota(jnp.int32, (ck, LANES), 1) < ck
    zk = jnp.zeros((ck, DN_DK), BF16)

    def pack_cols(arr, g, r0, r1):
        cols = [arr[r0:r1, g * HEAD_GROUP + i:g * HEAD_GROUP + i + 1] for i in range(HEAD_GROUP)]
        return jnp.where(ph == 0, cols[0], jnp.where(ph == 1, cols[1], jnp.where(ph == 2, cols[2], cols[3])))

    def diag_to_row(a):
        return jnp.sum(jnp.where(eye_pk, a, 0.0), axis=0, keepdims=True)

    items = [(g, c) for c in range(N_CHUNKS) for g in range(N_GROUPS)]

    a_pk, neg_l, da_pk, dec_pk, b_row, beg_row = {}, {}, {}, {}, {}, {}
    for (g, c) in items:
        r0, r1 = c * ck, (c + 1) * ck
        heads = [g * HEAD_GROUP + i for i in range(HEAD_GROUP)]
        q4 = jnp.concatenate([qb[hd][r0:r1] for hd in heads], axis=1)
        k4 = jnp.concatenate([kb[hd][r0:r1] for hd in heads], axis=1)
        bdk = jnp.concatenate(
            [jnp.concatenate([kb[hd][r0:r1] if i == j else zk for j in range(HEAD_GROUP)], axis=1)
             for i, hd in enumerate(heads)], axis=0)
        qkk = _dot_nt(jnp.concatenate([q4, k4], axis=0), bdk)
        gcol = pack_cols(gc, g, r0, r1)
        bcol = pack_cols(beta, g, r0, r1)
        grow = diag_to_row(gcol)
        brow = diag_to_row(bcol)
        gam = jnp.exp(jnp.where(incl_pk, gcol - grow, -jnp.inf))
        a_pk[(g, c)] = qkk[0:ck] * gam
        neg_l[(g, c)] = jnp.where(strict_pk, -(bcol * qkk[ck:] * gam), 0.0)
        da_pk[(g, c)] = jnp.where(eye_pk, jnp.exp(gcol), 0.0)
        dec_pk[(g, c)] = jnp.where(eye_pk, jnp.exp(gcol[ck - 1:ck, :] - gcol), 0.0)
        b_row[(g, c)] = brow
        beg_row[(g, c)] = brow * jnp.exp(grow)

    t_inv = {it: eye_f + neg_l[it] for it in items}
    pw = {}
    for it in items:
        pb = neg_l[it].astype(BF16)
        pw[it] = _dot(pb, _block_diag4(pb, lane_lo))
    for step in range(5):
        for it in items:
            pb = pw[it].astype(BF16)
            bd = _block_diag4(pb, lane_lo)
            if step < 4:
                res = _dot(jnp.concatenate([t_inv[it].astype(BF16), pb], axis=0), bd)
                t_inv[it] = t_inv[it] + res[0:ck]
                pw[it] = res[ck:]
            else:
                t_inv[it] = t_inv[it] + _dot(t_inv[it].astype(BF16), bd)

    u_all, w_all, kd_all = {}, {}, {}
    for (g, c) in items:
        r0, r1 = c * ck, (c + 1) * ck
        t_b = t_inv[(g, c)] * b_row[(g, c)]
        t_be = t_inv[(g, c)] * beg_row[(g, c)]
        for m in range(HEAD_GROUP // 2):
            p = g * (HEAD_GROUP // 2) + m
            h0, h1 = 2 * p, 2 * p + 1
            sl = slice(m * LANES, (m + 1) * LANES)
            u_all[(c, p)] = _dot(t_b[:, sl].astype(BF16), _pair_diag(vb[h0][r0:r1], vb[h1][r0:r1]))
            wkd = _dot(jnp.concatenate([t_be[:, sl], dec_pk[(g, c)][:, sl]], axis=0).astype(BF16),
                       _pair_diag(kb[h0][r0:r1], kb[h1][r0:r1]))
            w_all[(c, p)] = wkd[0:ck].astype(BF16)
            kd_all[(c, p)] = jnp.concatenate([wkd[ck:, 0:DN_DK], wkd[ck:, DN_DK:]], axis=0).astype(BF16)
    pc = jnp.concatenate(slabs[0:N_C_SLABS], axis=1)

    s_mat = [state_s[hd] for hd in range(DN_HEADS)]
    for c in range(N_CHUNKS):
        r0, r1 = c * ck, (c + 1) * ck
        ws = []
        for p in range(N_PAIRS):
            h0, h1 = 2 * p, 2 * p + 1
            q2 = jnp.concatenate([qb[h0][r0:r1], qb[h1][r0:r1]], axis=1)
            ws.append(_dot(jnp.concatenate([w_all[(c, p)], q2], axis=0),
                           _pair_diag(s_mat[h0].astype(BF16), s_mat[h1].astype(BF16))))
        issue_slab()
        for p in range(N_PAIRS):
            h0, h1 = 2 * p, 2 * p + 1
            g, m = p // (HEAD_GROUP // 2), p % (HEAD_GROUP // 2)
            sl = slice(m * LANES, (m + 1) * LANES)
            v_new = (u_all[(c, p)] - ws[p][0:ck]).astype(BF16)
            qs = ws[p][ck:].astype(BF16)
            bdv = _pair_diag(v_new[:, 0:DN_DV], v_new[:, DN_DV:])
            bdq = _pair_diag(qs[:, 0:DN_DV], qs[:, DN_DV:])
            lhs_o = jnp.concatenate([a_pk[(g, c)][:, sl], da_pk[(g, c)][:, sl]], axis=1).astype(BF16)
            o_s[r0:r1, h0 * DN_DV:(h1 + 1) * DN_DV] = _dot(lhs_o, jnp.concatenate([bdv, bdq], axis=0))
            upd = _dot_tn(kd_all[(c, p)], bdv)
            d0 = jnp.exp(gc[r1 - 1:r1, h0:h0 + 1])
            d1 = jnp.exp(gc[r1 - 1:r1, h1:h1 + 1])
            s_mat[h0] = d0 * s_mat[h0] + upd[:, 0:DN_DV]
            s_mat[h1] = d1 * s_mat[h1] + upd[:, DN_DV:]
        issue_slab()
    for hd in range(DN_HEADS):
        state_s[hd] = s_mat[hd]
    while len(slabs) < N_SLABS:
        issue_slab()
    late = jnp.concatenate(slabs[N_C_SLABS:], axis=1)
    z_b = _silu_of_half(late[:, 0:DN_HEADS * DN_DV])
    gates = late[:, DN_HEADS * DN_DV:]

    for hd in range(DN_HEADS):
        oh = o_s[:, hd * DN_DV:(hd + 1) * DN_DV]
        oh = oh * lax.rsqrt(jnp.mean(oh * oh, axis=-1, keepdims=True) + NORM_EPS) * ong_ref[...]
        o_s[:, hd * DN_DV:(hd + 1) * DN_DV] = oh
    y_b = (o_s[...] * z_b).astype(BF16)
    merged = _gate2_of_half(gates[:, D_MODEL:2 * D_MODEL]) * _dot(y_b, bproj_ref[...])
    merged = merged + _gate2_of_half(gates[:, 0:D_MODEL]) * _mixer_a(
        pa, ahist_s, adw_ref, adwb_ref, alng_ref, alnb_ref, aproj_ref)
    merged = merged + _gate2_of_half(gates[:, 2 * D_MODEL:]) * _mixer_c(
        pc, clng_ref, clnb_ref, cws_ref, cbst_ref, cproj_ref)

    y = x + _dot(merged.astype(BF16), wout_ref[...])
    if final_norm:
        y = y * lax.rsqrt(jnp.mean(y * y, axis=-1, keepdims=True) + NORM_EPS) * fg_ref[...]
    out_ref[0] = y


def _const_spec(shape):
    nd = len(shape)
    return pl.BlockSpec(shape, lambda b, t: (0,) * nd, pipeline_mode=pl.Buffered(1))


def _layer_call(x, params, final_g, final_norm):
    bsz, seq, d = x.shape
    assert d == D_MODEL and seq % SEQ_TILE == 0
    consts = list(params) + [final_g]
    tile_spec = pl.BlockSpec((1, SEQ_TILE, D_MODEL), lambda b, t: (b, t, 0))
    return pl.pallas_call(
        functools.partial(_layer_kernel, final_norm=final_norm),
        grid=(bsz, seq // SEQ_TILE),
        in_specs=[tile_spec] + [_const_spec(c.shape) for c in consts],
        out_specs=tile_spec,
        out_shape=jax.ShapeDtypeStruct(x.shape, F32),
        scratch_shapes=[
            pltpu.VMEM((SEQ_TILE + A_HALO, CONV_WIDTH), F32),
            pltpu.VMEM((SEQ_TILE + B_HALO, _QKV), F32),
            pltpu.VMEM((DN_HEADS, DN_DK, DN_DV), F32),
            pltpu.VMEM((SEQ_TILE, DN_HEADS * DN_DV), F32),
        ],
        compiler_params=pltpu.CompilerParams(
            dimension_semantics=("arbitrary", "arbitrary"),
            vmem_limit_bytes=VMEM_LIMIT_BYTES),
    )(x, *consts)


def _prep_layer(norm_g, w_in, a_dw, a_dw_b, a_ln_g, a_ln_b, a_proj, b_conv, b_a_log, b_dt_bias,
                b_onorm_g, b_proj, c_ln_g, c_ln_b, c_ws, c_bs, c_proj, w_out):
    o = 0
    pieces = {}
    for name, n in (("a", 3 * CONV_WIDTH), ("qkv", _QKV), ("bz", DN_HEADS * DN_DV),
                    ("ba", 2 * DN_HEADS), ("c", 3 * SG_WIDTH), ("g", 3 * D_MODEL)):
        pieces[name] = w_in[:, o:o + n]
        o += n
    c_half = jnp.concatenate([pieces["c"][:, 0:2 * SG_WIDTH], 0.5 * pieces["c"][:, 2 * SG_WIDTH:]], axis=1)
    w_main = jnp.concatenate([0.5 * pieces["a"], pieces["qkv"], c_half, 0.5 * pieces["bz"],
                              0.5 * pieces["g"]], axis=1).astype(BF16)
    row = lambda v: v.reshape(1, -1)
    return (row(norm_g), w_main, pieces["ba"].astype(BF16),
            a_dw, row(a_dw_b), row(0.5 * a_ln_g), row(0.5 * a_ln_b), (0.5 * a_proj).astype(BF16),
            0.5 * b_conv, row(b_a_log), row(b_dt_bias), row(b_onorm_g), (0.5 * b_proj).astype(BF16),
            row(c_ln_g), row(c_ln_b), c_ws, c_bs.T, (0.5 * c_proj).astype(BF16), w_out.astype(BF16))


def kernel(x, norm_g, w_in, a_dw, a_dw_b, a_ln_g, a_ln_b, a_proj, b_conv, b_a_log, b_dt_bias,
           b_onorm_g, b_proj, c_ln_g, c_ln_b, c_ws, c_bs, c_proj, w_out, final_g):
    depth = w_in.shape[0]
    fg = final_g.reshape(1, -1)
    for l in range(depth):
        params = _prep_layer(norm_g[l], w_in[l], a_dw[l], a_dw_b[l], a_ln_g[l], a_ln_b[l], a_proj[l],
                             b_conv[l], b_a_log[l], b_dt_bias[l], b_onorm_g[l], b_proj[l],
                             c_ln_g[l], c_ln_b[l], c_ws[l], c_bs[l], c_proj[l], w_out[l])
        x = _layer_call(x, params, fg, final_norm=(l == depth - 1))
    return x
```

```python
import functools

import jax
import jax.numpy as jnp
from jax import lax
from jax.experimental import pallas as pl
from jax.experimental.pallas import tpu as pltpu

D_MODEL = 1024
CONV_WIDTH = 512
CONV_K = 31
DN_DK = 128
DN_DV = 128
DN_HEADS = 8
DN_SHORT_K = 4
DN_CHUNK = 64
SG_WIDTH = 512
SG_GROUPS = 4
SG_CHUNK = 128
NORM_EPS = 1e-6

SEQ_TILE = 256
N_CHUNKS = SEQ_TILE // DN_CHUNK
A_HALO = 32
B_HALO = 8
SUBLANES = 8
LANES = 128
HEAD_GROUP = 4
N_GROUPS = DN_HEADS // HEAD_GROUP
N_PAIRS = DN_HEADS // 2
PACK_W = HEAD_GROUP * DN_CHUNK
SLAB = 512
VMEM_LIMIT_BYTES = 56 * 1024 * 1024

_QK = DN_HEADS * DN_DK
_QKV = DN_HEADS * (2 * DN_DK + DN_DV)
OFF_A = 0
OFF_QKV = OFF_A + 3 * CONV_WIDTH
OFF_C = OFF_QKV + _QKV
OFF_BZ = OFF_C + 3 * SG_WIDTH
OFF_G = OFF_BZ + DN_HEADS * DN_DV
N_MAIN = OFF_G + 3 * D_MODEL
N_C_SLABS = (OFF_BZ - OFF_C) // SLAB
N_SLABS = (N_MAIN - OFF_C) // SLAB

BF16 = jnp.bfloat16
F32 = jnp.float32


def _dot(a, b):
    return jnp.dot(a, b, preferred_element_type=F32)


def _dot_nt(a, b):
    return lax.dot_general(a, b, (((1,), (1,)), ((), ())), preferred_element_type=F32)


def _dot_tn(a, b):
    return lax.dot_general(a, b, (((0,), (0,)), ((), ())), preferred_element_type=F32)


def _weight(ref, lo=None, hi=None):
    packed = ref[...] if lo is None else ref[:, lo:hi]
    return pltpu.bitcast(packed, BF16)


def _sigmoid(x):
    return 1.0 / (1.0 + jnp.exp(-x))


def _silu_of_half(hx):
    return hx + hx * jnp.tanh(hx)


def _gate2_of_half(hx):
    return 1.0 + jnp.tanh(hx)


def _gelu_tanh(x):
    c = 0.7978845608028654
    return 0.5 * x * (1.0 + jnp.tanh(c * (x + 0.044715 * (x * x * x))))


def _softplus(x):
    return jnp.maximum(x, 0.0) + jnp.log1p(jnp.exp(-jnp.abs(x)))


def _layernorm(x, g, b):
    mu = jnp.mean(x, axis=-1, keepdims=True)
    xc = x - mu
    var = jnp.mean(xc * xc, axis=-1, keepdims=True)
    return xc * lax.rsqrt(var + NORM_EPS) * g + b


def _delayed(win, lag, rows):
    if lag:
        win = pltpu.roll(win, lag, axis=0)
    return win[SUBLANES:SUBLANES + rows]


def _pair_diag(a, b):
    z = jnp.zeros(a.shape, a.dtype)
    return jnp.concatenate([jnp.concatenate([a, z], axis=1), jnp.concatenate([z, b], axis=1)], axis=0)


def _block_diag4(x, lane_lo):
    zero = jnp.zeros((), x.dtype)
    z = jnp.zeros((x.shape[0], LANES), x.dtype)
    left, right = x[:, 0:LANES], x[:, LANES:]
    return jnp.concatenate([
        jnp.concatenate([jnp.where(lane_lo, left, zero), z], axis=1),
        jnp.concatenate([jnp.where(lane_lo, zero, left), z], axis=1),
        jnp.concatenate([z, jnp.where(lane_lo, right, zero)], axis=1),
        jnp.concatenate([z, jnp.where(lane_lo, zero, right)], axis=1)], axis=0)


def _mixer_a(pa, ahist_s, adw_ref, adwb_ref, alng_ref, alnb_ref, aproj_ref):
    tt = SEQ_TILE
    ahist_s[A_HALO:A_HALO + tt, :] = pa[:, 0:CONV_WIDTH] * _gate2_of_half(pa[:, CONV_WIDTH:2 * CONV_WIDTH])
    win0 = A_HALO - SUBLANES
    acc = None
    for r in range(SUBLANES):
        part = None
        for m in range((CONV_K - 1 - r) // SUBLANES + 1):
            tap = CONV_K - 1 - (SUBLANES * m + r)
            lo = win0 - SUBLANES * m
            term = adw_ref[tap:tap + 1, :] * ahist_s[lo:lo + tt + SUBLANES, :]
            part = term if part is None else part + term
        part = _delayed(part, r, tt)
        acc = part if acc is None else acc + part
    ahist_s[0:A_HALO, :] = ahist_s[tt:tt + A_HALO, :]
    a = _layernorm(acc + adwb_ref[...], alng_ref[...], alnb_ref[...])
    y_a = (_silu_of_half(a) * _silu_of_half(pa[:, 2 * CONV_WIDTH:])).astype(BF16)
    return _dot(y_a, _weight(aproj_ref))


def _mixer_c(pc, clng_ref, clnb_ref, cws_ref, cbst_ref, cproj_ref):
    tt = SEQ_TILE
    u_c = _gelu_tanh(pc[:, 0:SG_WIDTH])
    v_c = _layernorm(_gelu_tanh(pc[:, SG_WIDTH:2 * SG_WIDTH]), clng_ref[...], clnb_ref[...])
    v_cb = v_c.astype(BF16)
    z_c = _silu_of_half(pc[:, 2 * SG_WIDTH:])
    si = lax.broadcasted_iota(jnp.int32, (SG_CHUNK, SG_CHUNK), 0)
    sj = lax.broadcasted_iota(jnp.int32, (SG_CHUNK, SG_CHUNK), 1)
    gw = SG_WIDTH // SG_GROUPS
    group_cols = []
    for gidx in range(SG_GROUPS):
        w_tri = jnp.where(si >= sj, cws_ref[gidx], 0.0).astype(BF16)
        bias = cbst_ref[:, gidx:gidx + 1]
        rows = []
        for n in range(tt // SG_CHUNK):
            blk = v_cb[n * SG_CHUNK:(n + 1) * SG_CHUNK, gidx * gw:(gidx + 1) * gw]
            rows.append(_dot(w_tri, blk) + bias)
        group_cols.append(jnp.concatenate(rows, axis=0))
    mixed = jnp.concatenate(group_cols, axis=1)
    y_c = (u_c * mixed * z_c).astype(BF16)
    return _dot(y_c, _weight(cproj_ref))


def _layer_kernel(x_ref, ng_ref, wmain_ref, wba_ref,
                  adw_ref, adwb_ref, alng_ref, alnb_ref, aproj_ref,
                  bconv_ref, alog_ref, dtb_ref, ong_ref, bproj_ref,
                  clng_ref, clnb_ref, cws_ref, cbst_ref, cproj_ref, wout_ref, fg_ref,
                  out_ref,
                  ahist_s, qhist_s, state_s, o_s, *, final_norm):
    tt = SEQ_TILE
    ck = DN_CHUNK

    @pl.when(pl.program_id(1) == 0)
    def _reset():
        ahist_s[0:A_HALO, :] = jnp.zeros((A_HALO, CONV_WIDTH), F32)
        qhist_s[0:B_HALO, :] = jnp.zeros((B_HALO, _QKV), F32)
        state_s[...] = jnp.zeros(state_s.shape, F32)

    x = x_ref[0]
    h = x * lax.rsqrt(jnp.mean(x * x, axis=-1, keepdims=True) + NORM_EPS) * ng_ref[...]
    hb = h.astype(BF16)

    qhist_s[B_HALO:B_HALO + tt, 0:2 * _QK] = _dot(hb, _weight(wmain_ref, OFF_QKV, OFF_QKV + 2 * _QK))
    ba = _dot(hb, _weight(wba_ref))

    queue = ([OFF_QKV + 2 * _QK + i * SLAB for i in range(DN_HEADS * DN_DV // SLAB)]
             + [OFF_A + i * SLAB for i in range(3 * CONV_WIDTH // SLAB)]
             + [OFF_C + i * SLAB for i in range(N_SLABS)])
    issued = {}

    def issue_slab(count=1):
        for _ in range(count):
            if queue:
                col = queue.pop(0)
                issued[col] = _dot(hb, _weight(wmain_ref, col, col + SLAB))

    def take(col0, width):
        return jnp.concatenate([issued[col0 + i * SLAB] for i in range(width // SLAB)], axis=1)

    def short_conv(c0, c1):
        qwin = qhist_s[:, c0:c1]
        conv = None
        for lag in range(DN_SHORT_K):
            term = bconv_ref[DN_SHORT_K - 1 - lag:DN_SHORT_K - lag, c0:c1] * _delayed(qwin, lag, tt)
            conv = term if conv is None else conv + term
        qhist_s[0:B_HALO, c0:c1] = qhist_s[tt:tt + B_HALO, c0:c1]
        return _silu_of_half(conv)

    issue_slab(8)
    qk_act = short_conv(0, 2 * _QK)

    beta = _sigmoid(ba[:, 0:DN_HEADS])
    g_col = -jnp.exp(alog_ref[...]) * _softplus(ba[:, DN_HEADS:2 * DN_HEADS] + dtb_ref[...])

    ri = lax.broadcasted_iota(jnp.int32, (tt, tt), 0)
    ci = lax.broadcasted_iota(jnp.int32, (tt, tt), 1)
    lo_tri = jnp.where(((ri >> 6) == (ci >> 6)) & (ci <= ri), 1.0, 0.0).astype(BF16)
    gch = g_col.astype(BF16)
    gcl = (g_col - gch.astype(F32)).astype(BF16)
    gc = _dot(lo_tri, gch) + _dot(lo_tri, gcl)

    q_scale = DN_DK ** -0.5
    qb, kb = [], []
    for hd in range(DN_HEADS):
        qh = qk_act[:, hd * DN_DK:(hd + 1) * DN_DK]
        kh = qk_act[:, _QK + hd * DN_DK:_QK + (hd + 1) * DN_DK]
        qb.append((qh * (lax.rsqrt(jnp.sum(qh * qh, axis=-1, keepdims=True) + NORM_EPS) * q_scale)).astype(BF16))
        kb.append((kh * lax.rsqrt(jnp.sum(kh * kh, axis=-1, keepdims=True) + NORM_EPS)).astype(BF16))

    pi = lax.broadcasted_iota(jnp.int32, (ck, PACK_W), 0)
    pl_ = lax.broadcasted_iota(jnp.int32, (ck, PACK_W), 1)
    pj = pl_ & (ck - 1)
    ph = pl_ >> 6
    incl_pk = pi >= pj
    strict_pk = pi > pj
    eye_pk = pi == pj
    eye_f = jnp.where(eye_pk, 1.0, 0.0).astype(F32)
    lane_lo = lax.broadcasted_iota(jnp.int32, (ck, LANES), 1) < ck
    zk = jnp.zeros((ck, DN_DK), BF16)

    def pack_cols(arr, g, r0, r1):
        cols = [arr[r0:r1, g * HEAD_GROUP + i:g * HEAD_GROUP + i + 1] for i in range(HEAD_GROUP)]
        return jnp.where(ph == 0, cols[0], jnp.where(ph == 1, cols[1], jnp.where(ph == 2, cols[2], cols[3])))

    def diag_to_row(a):
        return jnp.sum(jnp.where(eye_pk, a, 0.0), axis=0, keepdims=True)

    items = [(g, c) for c in range(N_CHUNKS) for g in range(N_GROUPS)]

    a_pk, neg_l, da_pk, dec_pk, b_row, beg_row = {}, {}, {}, {}, {}, {}
    for (g, c) in items:
        r0, r1 = c * ck, (c + 1) * ck
        heads = [g * HEAD_GROUP + i for i in range(HEAD_GROUP)]
        q4 = jnp.concatenate([qb[hd][r0:r1] for hd in heads], axis=1)
        k4 = jnp.concatenate([kb[hd][r0:r1] for hd in heads], axis=1)
        bdk = jnp.concatenate(
            [jnp.concatenate([kb[hd][r0:r1] if i == j else zk for j in range(HEAD_GROUP)], axis=1)
             for i, hd in enumerate(heads)], axis=0)
        qkk = _dot_nt(jnp.concatenate([q4, k4], axis=0), bdk)
        gcol = pack_cols(gc, g, r0, r1)
        bcol = pack_cols(beta, g, r0, r1)
        grow = diag_to_row(gcol)
        brow = diag_to_row(bcol)
        gam = jnp.exp(jnp.where(incl_pk, gcol - grow, -jnp.inf))
        a_pk[(g, c)] = qkk[0:ck] * gam
        neg_l[(g, c)] = jnp.where(strict_pk, -(bcol * qkk[ck:] * gam), 0.0)
        da_pk[(g, c)] = jnp.where(eye_pk, jnp.exp(gcol), 0.0)
        dec_pk[(g, c)] = jnp.where(eye_pk, jnp.exp(gcol[ck - 1:ck, :] - gcol), 0.0)
        b_row[(g, c)] = brow
        beg_row[(g, c)] = brow * jnp.exp(grow)

    t_inv = {it: eye_f + neg_l[it] for it in items}
    pw = {}
    for it in items:
        pb = neg_l[it].astype(BF16)
        pw[it] = _dot(pb, _block_diag4(pb, lane_lo))
    for step in range(5):
        for it in items:
            pb = pw[it].astype(BF16)
            bd = _block_diag4(pb, lane_lo)
            if step < 4:
                res = _dot(jnp.concatenate([t_inv[it].astype(BF16), pb], axis=0), bd)
                t_inv[it] = t_inv[it] + res[0:ck]
                pw[it] = res[ck:]
            else:
                t_inv[it] = t_inv[it] + _dot(t_inv[it].astype(BF16), bd)
    v_off = OFF_QKV + 2 * _QK
    qhist_s[B_HALO:B_HALO + tt, 2 * _QK:] = take(v_off, DN_HEADS * DN_DV)
    v_act = short_conv(2 * _QK, _QKV)
    vb = [v_act[:, hd * DN_DV:(hd + 1) * DN_DV].astype(BF16) for hd in range(DN_HEADS)]
    pa = take(OFF_A, 3 * CONV_WIDTH)
    pc = take(OFF_C, 3 * SG_WIDTH)

    u_all, w_all, kd_all = {}, {}, {}
    for (g, c) in items:
        r0, r1 = c * ck, (c + 1) * ck
        t_b = t_inv[(g, c)] * b_row[(g, c)]
        t_be = t_inv[(g, c)] * beg_row[(g, c)]
        for m in range(HEAD_GROUP // 2):
            p = g * (HEAD_GROUP // 2) + m
            h0, h1 = 2 * p, 2 * p + 1
            sl = slice(m * LANES, (m + 1) * LANES)
            u_all[(c, p)] = _dot(t_b[:, sl].astype(BF16), _pair_diag(vb[h0][r0:r1], vb[h1][r0:r1]))
            wkd = _dot(jnp.concatenate([t_be[:, sl], dec_pk[(g, c)][:, sl]], axis=0).astype(BF16),
                       _pair_diag(kb[h0][r0:r1], kb[h1][r0:r1]))
            w_all[(c, p)] = wkd[0:ck].astype(BF16)
            kd_all[(c, p)] = jnp.concatenate([wkd[ck:, 0:DN_DK], wkd[ck:, DN_DK:]], axis=0).astype(BF16)

    s_mat = [state_s[hd] for hd in range(DN_HEADS)]
    for c in range(N_CHUNKS):
        r0, r1 = c * ck, (c + 1) * ck
        ws = []
        for p in range(N_PAIRS):
            h0, h1 = 2 * p, 2 * p + 1
            q2 = jnp.concatenate([qb[h0][r0:r1], qb[h1][r0:r1]], axis=1)
            ws.append(_dot(jnp.concatenate([w_all[(c, p)], q2], axis=0),
                           _pair_diag(s_mat[h0].astype(BF16), s_mat[h1].astype(BF16))))
        issue_slab()
        for p in range(N_PAIRS):
            h0, h1 = 2 * p, 2 * p + 1
            g, m = p // (HEAD_GROUP // 2), p % (HEAD_GROUP // 2)
            sl = slice(m * LANES, (m + 1) * LANES)
            v_new = (u_all[(c, p)] - ws[p][0:ck]).astype(BF16)
            qs = ws[p][ck:].astype(BF16)
            bdv = _pair_diag(v_new[:, 0:DN_DV], v_new[:, DN_DV:])
            bdq = _pair_diag(qs[:, 0:DN_DV], qs[:, DN_DV:])
            lhs_o = jnp.concatenate([a_pk[(g, c)][:, sl], da_pk[(g, c)][:, sl]], axis=1).astype(BF16)
            o_s[r0:r1, h0 * DN_DV:(h1 + 1) * DN_DV] = _dot(lhs_o, jnp.concatenate([bdv, bdq], axis=0))
            upd = _dot_tn(kd_all[(c, p)], bdv)
            d0 = jnp.exp(gc[r1 - 1:r1, h0:h0 + 1])
            d1 = jnp.exp(gc[r1 - 1:r1, h1:h1 + 1])
            s_mat[h0] = d0 * s_mat[h0] + upd[:, 0:DN_DV]
            s_mat[h1] = d1 * s_mat[h1] + upd[:, DN_DV:]
        issue_slab()
    for hd in range(DN_HEADS):
        state_s[hd] = s_mat[hd]
    issue_slab(len(queue))
    late = take(OFF_BZ, N_MAIN - OFF_BZ)
    z_b = _silu_of_half(late[:, 0:DN_HEADS * DN_DV])
    gates = late[:, DN_HEADS * DN_DV:]

    for hd in range(DN_HEADS):
        oh = o_s[:, hd * DN_DV:(hd + 1) * DN_DV]
        oh = oh * lax.rsqrt(jnp.mean(oh * oh, axis=-1, keepdims=True) + NORM_EPS) * ong_ref[...]
        o_s[:, hd * DN_DV:(hd + 1) * DN_DV] = oh
    y_b = (o_s[...] * z_b).astype(BF16)
    merged = _gate2_of_half(gates[:, D_MODEL:2 * D_MODEL]) * _dot(y_b, _weight(bproj_ref))
    merged = merged + _gate2_of_half(gates[:, 0:D_MODEL]) * _mixer_a(
        pa, ahist_s, adw_ref, adwb_ref, alng_ref, alnb_ref, aproj_ref)
    merged = merged + _gate2_of_half(gates[:, 2 * D_MODEL:]) * _mixer_c(
        pc, clng_ref, clnb_ref, cws_ref, cbst_ref, cproj_ref)

    y = x + _dot(merged.astype(BF16), _weight(wout_ref))
    if final_norm:
        y = y * lax.rsqrt(jnp.mean(y * y, axis=-1, keepdims=True) + NORM_EPS) * fg_ref[...]
    out_ref[0] = y


def _const_spec(shape):
    nd = len(shape)
    return pl.BlockSpec(shape, lambda b, t: (0,) * nd, pipeline_mode=pl.Buffered(1))


def _layer_call(x, params, final_g, final_norm):
    bsz, seq, d = x.shape
    assert d == D_MODEL and seq % SEQ_TILE == 0
    consts = list(params) + [final_g]
    tile_spec = pl.BlockSpec((1, SEQ_TILE, D_MODEL), lambda b, t: (b, t, 0))
    return pl.pallas_call(
        functools.partial(_layer_kernel, final_norm=final_norm),
        grid=(bsz, seq // SEQ_TILE),
        in_specs=[tile_spec] + [_const_spec(c.shape) for c in consts],
        out_specs=tile_spec,
        out_shape=jax.ShapeDtypeStruct(x.shape, F32),
        scratch_shapes=[
            pltpu.VMEM((SEQ_TILE + A_HALO, CONV_WIDTH), F32),
            pltpu.VMEM((SEQ_TILE + B_HALO, _QKV), F32),
            pltpu.VMEM((DN_HEADS, DN_DK, DN_DV), F32),
            pltpu.VMEM((SEQ_TILE, DN_HEADS * DN_DV), F32),
        ],
        compiler_params=pltpu.CompilerParams(
            dimension_semantics=("arbitrary", "arbitrary"),
            vmem_limit_bytes=VMEM_LIMIT_BYTES),
    )(x, *consts)


def _pack_rows(w):
    bits = lax.bitcast_convert_type(w.astype(BF16), jnp.uint16).astype(jnp.uint32)
    return bits[0::2] | (bits[1::2] << 16)


def _prep_layer(norm_g, w_in, a_dw, a_dw_b, a_ln_g, a_ln_b, a_proj, b_conv, b_a_log, b_dt_bias,
                b_onorm_g, b_proj, c_ln_g, c_ln_b, c_ws, c_bs, c_proj, w_out):
    o = 0
    pieces = {}
    for name, n in (("a", 3 * CONV_WIDTH), ("qkv", _QKV), ("bz", DN_HEADS * DN_DV),
                    ("ba", 2 * DN_HEADS), ("c", 3 * SG_WIDTH), ("g", 3 * D_MODEL)):
        pieces[name] = w_in[:, o:o + n]
        o += n
    c_half = jnp.concatenate([pieces["c"][:, 0:2 * SG_WIDTH], 0.5 * pieces["c"][:, 2 * SG_WIDTH:]], axis=1)
    w_main = _pack_rows(jnp.concatenate([0.5 * pieces["a"], pieces["qkv"], c_half, 0.5 * pieces["bz"],
                                         0.5 * pieces["g"]], axis=1))
    row = lambda v: v.reshape(1, -1)
    return (row(norm_g), w_main, _pack_rows(pieces["ba"]),
            a_dw, row(a_dw_b), row(0.5 * a_ln_g), row(0.5 * a_ln_b), _pack_rows(0.5 * a_proj),
            0.5 * b_conv, row(b_a_log), row(b_dt_bias), row(b_onorm_g), _pack_rows(0.5 * b_proj),
            row(c_ln_g), row(c_ln_b), c_ws, c_bs.T, _pack_rows(0.5 * c_proj), _pack_rows(w_out))


def kernel(x, norm_g, w_in, a_dw, a_dw_b, a_ln_g, a_ln_b, a_proj, b_conv, b_a_log, b_dt_bias,
           b_onorm_g, b_proj, c_ln_g, c_ln_b, c_ws, c_bs, c_proj, w_out, final_g):
    depth = w_in.shape[0]
    fg = final_g.reshape(1, -1)
    for l in range(depth):
        params = _prep_layer(norm_g[l], w_in[l], a_dw[l], a_dw_b[l], a_ln_g[l], a_ln_b[l], a_proj[l],
                             b_conv[l], b_a_log[l], b_dt_bias[l], b_onorm_g[l], b_proj[l],
                             c_ln_g[l], c_ln_b[l], c_ws[l], c_bs[l], c_proj[l], w_out[l])
        x = _layer_call(x, params, fg, final_norm=(l == depth - 1))
    return x
```

```python
import functools

import jax
import jax.numpy as jnp
import numpy as np
from jax import lax
from jax.experimental import pallas as pl
from jax.experimental.pallas import tpu as pltpu

D_MODEL = 1024
CONV_WIDTH = 512
CONV_K = 31
DN_DK = 128
DN_DV = 128
DN_HEADS = 8
DN_SHORT_K = 4
DN_CHUNK = 64
SG_WIDTH = 512
SG_GROUPS = 4
SG_CHUNK = 128
NORM_EPS = 1e-6

SEQ_TILE = 256
N_CHUNKS = SEQ_TILE // DN_CHUNK
A_HALO = 32
B_HALO = 8
SUBLANES = 8
LANES = 128
HEAD_GROUP = 4
N_GROUPS = DN_HEADS // HEAD_GROUP
N_PAIRS = DN_HEADS // 2
PACK_W = HEAD_GROUP * DN_CHUNK
SLAB = 512
VMEM_LIMIT_BYTES = 56 * 1024 * 1024

_QK = DN_HEADS * DN_DK
_QKV = DN_HEADS * (2 * DN_DK + DN_DV)
OFF_A = 0
OFF_QKV = OFF_A + 3 * CONV_WIDTH
OFF_C = OFF_QKV + _QKV
OFF_BZ = OFF_C + 3 * SG_WIDTH
OFF_G = OFF_BZ + DN_HEADS * DN_DV
N_MAIN = OFF_G + 3 * D_MODEL
N_C_SLABS = (OFF_BZ - OFF_C) // SLAB
N_SLABS = (N_MAIN - OFF_C) // SLAB

BF16 = jnp.bfloat16
F32 = jnp.float32


def _dot(a, b):
    return jnp.dot(a, b, preferred_element_type=F32)


def _dot_nt(a, b):
    return lax.dot_general(a, b, (((1,), (1,)), ((), ())), preferred_element_type=F32)


def _dot_tn(a, b):
    return lax.dot_general(a, b, (((0,), (0,)), ((), ())), preferred_element_type=F32)


def _weight(ref, lo=None, hi=None):
    packed = ref[...] if lo is None else ref[:, lo:hi]
    return pltpu.bitcast(packed, BF16)


def _sigmoid(x):
    return 1.0 / (1.0 + jnp.exp(-x))


def _silu_of_half(hx):
    return hx + hx * jnp.tanh(hx)


def _gate2_of_half(hx):
    return 1.0 + jnp.tanh(hx)


def _gelu_tanh(x):
    c = 0.7978845608028654
    return 0.5 * x * (1.0 + jnp.tanh(c * (x + 0.044715 * (x * x * x))))


def _softplus(x):
    return jnp.maximum(x, 0.0) + jnp.log1p(jnp.exp(-jnp.abs(x)))


def _layernorm(x, g, b):
    mu = jnp.mean(x, axis=-1, keepdims=True)
    xc = x - mu
    var = jnp.mean(xc * xc, axis=-1, keepdims=True)
    return xc * lax.rsqrt(var + NORM_EPS) * g + b


def _delayed(win, lag, rows):
    if lag:
        win = pltpu.roll(win, lag, axis=0)
    return win[SUBLANES:SUBLANES + rows]


def _pair_diag(a, b):
    z = jnp.zeros(a.shape, a.dtype)
    return jnp.concatenate([jnp.concatenate([a, z], axis=1), jnp.concatenate([z, b], axis=1)], axis=0)


def _block_diag4(x, lane_lo):
    zero = jnp.zeros((), x.dtype)
    z = jnp.zeros((x.shape[0], LANES), x.dtype)
    left, right = x[:, 0:LANES], x[:, LANES:]
    return jnp.concatenate([
        jnp.concatenate([jnp.where(lane_lo, left, zero), z], axis=1),
        jnp.concatenate([jnp.where(lane_lo, zero, left), z], axis=1),
        jnp.concatenate([z, jnp.where(lane_lo, right, zero)], axis=1),
        jnp.concatenate([z, jnp.where(lane_lo, zero, right)], axis=1)], axis=0)


def _mixer_a(pa, ahist_s, adw_ref, adwb_ref, alng_ref, alnb_ref, aproj_ref):
    tt = SEQ_TILE
    ahist_s[A_HALO:A_HALO + tt, :] = pa[:, 0:CONV_WIDTH] * _gate2_of_half(pa[:, CONV_WIDTH:2 * CONV_WIDTH])
    win0 = A_HALO - SUBLANES
    acc = None
    for r in range(SUBLANES):
        part = None
        for m in range((CONV_K - 1 - r) // SUBLANES + 1):
            tap = CONV_K - 1 - (SUBLANES * m + r)
            lo = win0 - SUBLANES * m
            term = adw_ref[tap:tap + 1, :] * ahist_s[lo:lo + tt + SUBLANES, :]
            part = term if part is None else part + term
        part = _delayed(part, r, tt)
        acc = part if acc is None else acc + part
    ahist_s[0:A_HALO, :] = ahist_s[tt:tt + A_HALO, :]
    a = _layernorm(acc + adwb_ref[...], alng_ref[...], alnb_ref[...])
    y_a = (_silu_of_half(a) * _silu_of_half(pa[:, 2 * CONV_WIDTH:])).astype(BF16)
    return _dot(y_a, _weight(aproj_ref))


def _mixer_c(pc, clng_ref, clnb_ref, cws_ref, cbst_ref, cproj_ref):
    tt = SEQ_TILE
    u_c = _gelu_tanh(pc[:, 0:SG_WIDTH])
    v_c = _layernorm(_gelu_tanh(pc[:, SG_WIDTH:2 * SG_WIDTH]), clng_ref[...], clnb_ref[...])
    v_cb = v_c.astype(BF16)
    z_c = _silu_of_half(pc[:, 2 * SG_WIDTH:])
    si = lax.broadcasted_iota(jnp.int32, (SG_CHUNK, SG_CHUNK), 0)
    sj = lax.broadcasted_iota(jnp.int32, (SG_CHUNK, SG_CHUNK), 1)
    gw = SG_WIDTH // SG_GROUPS
    group_cols = []
    for gidx in range(SG_GROUPS):
        w_tri = jnp.where(si >= sj, cws_ref[gidx], 0.0).astype(BF16)
        bias = cbst_ref[:, gidx:gidx + 1]
        rows = []
        for n in range(tt // SG_CHUNK):
            blk = v_cb[n * SG_CHUNK:(n + 1) * SG_CHUNK, gidx * gw:(gidx + 1) * gw]
            rows.append(_dot(w_tri, blk) + bias)
        group_cols.append(jnp.concatenate(rows, axis=0))
    mixed = jnp.concatenate(group_cols, axis=1)
    y_c = (u_c * mixed * z_c).astype(BF16)
    return _dot(y_c, _weight(cproj_ref))


def _layer_kernel(x_ref, ng_ref, wmain_ref, wba_ref,
                  adw_ref, adwb_ref, alng_ref, alnb_ref, aproj_ref,
                  bconv_ref, alog_ref, dtb_ref, ong_ref, bproj_ref,
                  clng_ref, clnb_ref, cws_ref, cbst_ref, cproj_ref, wout_ref, fg_ref,
                  out_ref,
                  ahist_s, qhist_s, state_s, o_s, *, final_norm):
    tt = SEQ_TILE
    ck = DN_CHUNK

    @pl.when(pl.program_id(1) == 0)
    def _reset():
        ahist_s[0:A_HALO, :] = jnp.zeros((A_HALO, CONV_WIDTH), F32)
        qhist_s[0:B_HALO, :] = jnp.zeros((B_HALO, _QKV), F32)
        state_s[...] = jnp.zeros(state_s.shape, F32)

    x = x_ref[0]
    h = x * lax.rsqrt(jnp.mean(x * x, axis=-1, keepdims=True) + NORM_EPS) * ng_ref[...]
    hb = h.astype(BF16)

    qhist_s[B_HALO:B_HALO + tt, 0:2 * _QK] = _dot(hb, _weight(wmain_ref, OFF_QKV, OFF_QKV + 2 * _QK))
    ba = _dot(hb, _weight(wba_ref))

    queue = ([OFF_QKV + 2 * _QK + i * SLAB for i in range(DN_HEADS * DN_DV // SLAB)]
             + [OFF_A + i * SLAB for i in range(3 * CONV_WIDTH // SLAB)]
             + [OFF_C + i * SLAB for i in range(N_SLABS)])
    issued = {}

    def issue_slab(count=1):
        for _ in range(count):
            if queue:
                col = queue.pop(0)
                issued[col] = _dot(hb, _weight(wmain_ref, col, col + SLAB))

    def take(col0, width):
        return jnp.concatenate([issued[col0 + i * SLAB] for i in range(width // SLAB)], axis=1)

    def short_conv(c0, c1):
        qwin = qhist_s[:, c0:c1]
        conv = None
        for lag in range(DN_SHORT_K):
            term = bconv_ref[DN_SHORT_K - 1 - lag:DN_SHORT_K - lag, c0:c1] * _delayed(qwin, lag, tt)
            conv = term if conv is None else conv + term
        qhist_s[0:B_HALO, c0:c1] = qhist_s[tt:tt + B_HALO, c0:c1]
        return _silu_of_half(conv)

    issue_slab(8)
    qk_act = short_conv(0, 2 * _QK)

    beta = _sigmoid(ba[:, 0:DN_HEADS])
    g_col = -jnp.exp(alog_ref[...]) * _softplus(ba[:, DN_HEADS:2 * DN_HEADS] + dtb_ref[...])

    ri = lax.broadcasted_iota(jnp.int32, (tt, tt), 0)
    ci = lax.broadcasted_iota(jnp.int32, (tt, tt), 1)
    lo_tri = jnp.where(((ri >> 6) == (ci >> 6)) & (ci <= ri), 1.0, 0.0).astype(BF16)
    gch = g_col.astype(BF16)
    gcl = (g_col - gch.astype(F32)).astype(BF16)
    gc = _dot(lo_tri, gch) + _dot(lo_tri, gcl)

    q_scale = DN_DK ** -0.5
    qb, kb = [], []
    for hd in range(DN_HEADS):
        qh = qk_act[:, hd * DN_DK:(hd + 1) * DN_DK]
        kh = qk_act[:, _QK + hd * DN_DK:_QK + (hd + 1) * DN_DK]
        qb.append((qh * (lax.rsqrt(jnp.sum(qh * qh, axis=-1, keepdims=True) + NORM_EPS) * q_scale)).astype(BF16))
        kb.append((kh * lax.rsqrt(jnp.sum(kh * kh, axis=-1, keepdims=True) + NORM_EPS)).astype(BF16))

    pi = lax.broadcasted_iota(jnp.int32, (ck, PACK_W), 0)
    pl_ = lax.broadcasted_iota(jnp.int32, (ck, PACK_W), 1)
    pj = pl_ & (ck - 1)
    ph = pl_ >> 6
    incl_pk = pi >= pj
    strict_pk = pi > pj
    eye_pk = pi == pj
    eye_f = jnp.where(eye_pk, 1.0, 0.0).astype(F32)
    lane_lo = lax.broadcasted_iota(jnp.int32, (ck, LANES), 1) < ck
    zk = jnp.zeros((ck, DN_DK), BF16)

    def pack_cols(arr, g, r0, r1):
        cols = [arr[r0:r1, g * HEAD_GROUP + i:g * HEAD_GROUP + i + 1] for i in range(HEAD_GROUP)]
        return jnp.where(ph == 0, cols[0], jnp.where(ph == 1, cols[1], jnp.where(ph == 2, cols[2], cols[3])))

    def diag_to_row(a):
        return jnp.sum(jnp.where(eye_pk, a, 0.0), axis=0, keepdims=True)

    items = [(g, c) for c in range(N_CHUNKS) for g in range(N_GROUPS)]

    a_pk, neg_l, da_pk, dec_pk, b_row, beg_row = {}, {}, {}, {}, {}, {}
    for (g, c) in items:
        r0, r1 = c * ck, (c + 1) * ck
        heads = [g * HEAD_GROUP + i for i in range(HEAD_GROUP)]
        q4 = jnp.concatenate([qb[hd][r0:r1] for hd in heads], axis=1)
        k4 = jnp.concatenate([kb[hd][r0:r1] for hd in heads], axis=1)
        bdk = jnp.concatenate(
            [jnp.concatenate([kb[hd][r0:r1] if i == j else zk for j in range(HEAD_GROUP)], axis=1)
             for i, hd in enumerate(heads)], axis=0)
        qkk = _dot_nt(jnp.concatenate([q4, k4], axis=0), bdk)
        gcol = pack_cols(gc, g, r0, r1)
        bcol = pack_cols(beta, g, r0, r1)
        grow = diag_to_row(gcol)
        brow = diag_to_row(bcol)
        gam = jnp.exp(jnp.where(incl_pk, gcol - grow, -jnp.inf))
        a_pk[(g, c)] = qkk[0:ck] * gam
        neg_l[(g, c)] = jnp.where(strict_pk, -(bcol * qkk[ck:] * gam), 0.0)
        da_pk[(g, c)] = jnp.where(eye_pk, jnp.exp(gcol), 0.0)
        dec_pk[(g, c)] = jnp.where(eye_pk, jnp.exp(gcol[ck - 1:ck, :] - gcol), 0.0)
        b_row[(g, c)] = brow
        beg_row[(g, c)] = brow * jnp.exp(grow)

    t_inv = {it: eye_f + neg_l[it] for it in items}
    pw = {}
    for it in items:
        pb = neg_l[it].astype(BF16)
        pw[it] = _dot(pb, _block_diag4(pb, lane_lo))
    for step in range(5):
        for it in items:
            pb = pw[it].astype(BF16)
            bd = _block_diag4(pb, lane_lo)
            if step < 4:
                res = _dot(jnp.concatenate([t_inv[it].astype(BF16), pb], axis=0), bd)
                t_inv[it] = t_inv[it] + res[0:ck]
                pw[it] = res[ck:]
            else:
                t_inv[it] = t_inv[it] + _dot(t_inv[it].astype(BF16), bd)
    v_off = OFF_QKV + 2 * _QK
    qhist_s[B_HALO:B_HALO + tt, 2 * _QK:] = take(v_off, DN_HEADS * DN_DV)
    v_act = short_conv(2 * _QK, _QKV)
    vb = [v_act[:, hd * DN_DV:(hd + 1) * DN_DV].astype(BF16) for hd in range(DN_HEADS)]
    pa = take(OFF_A, 3 * CONV_WIDTH)
    pc = take(OFF_C, 3 * SG_WIDTH)

    u_all, w_all, kd_all = {}, {}, {}
    for (g, c) in items:
        r0, r1 = c * ck, (c + 1) * ck
        t_b = t_inv[(g, c)] * b_row[(g, c)]
        t_be = t_inv[(g, c)] * beg_row[(g, c)]
        for m in range(HEAD_GROUP // 2):
            p = g * (HEAD_GROUP // 2) + m
            h0, h1 = 2 * p, 2 * p + 1
            sl = slice(m * LANES, (m + 1) * LANES)
            u_all[(c, p)] = _dot(t_b[:, sl].astype(BF16), _pair_diag(vb[h0][r0:r1], vb[h1][r0:r1]))
            wkd = _dot(jnp.concatenate([t_be[:, sl], dec_pk[(g, c)][:, sl]], axis=0).astype(BF16),
                       _pair_diag(kb[h0][r0:r1], kb[h1][r0:r1]))
            w_all[(c, p)] = wkd[0:ck].astype(BF16)
            kd_all[(c, p)] = jnp.concatenate([wkd[ck:, 0:DN_DK], wkd[ck:, DN_DK:]], axis=0).astype(BF16)

    s_mat = [state_s[hd] for hd in range(DN_HEADS)]
    for c in range(N_CHUNKS):
        r0, r1 = c * ck, (c + 1) * ck
        ws = []
        for p in range(N_PAIRS):
            h0, h1 = 2 * p, 2 * p + 1
            q2 = jnp.concatenate([qb[h0][r0:r1], qb[h1][r0:r1]], axis=1)
            ws.append(_dot(jnp.concatenate([w_all[(c, p)], q2], axis=0),
                           _pair_diag(s_mat[h0].astype(BF16), s_mat[h1].astype(BF16))))
        issue_slab()
        for p in range(N_PAIRS):
            h0, h1 = 2 * p, 2 * p + 1
            g, m = p // (HEAD_GROUP // 2), p % (HEAD_GROUP // 2)
            sl = slice(m * LANES, (m + 1) * LANES)
            v_new = (u_all[(c, p)] - ws[p][0:ck]).astype(BF16)
            qs = ws[p][ck:].astype(BF16)
            bdv = _pair_diag(v_new[:, 0:DN_DV], v_new[:, DN_DV:])
            bdq = _pair_diag(qs[:, 0:DN_DV], qs[:, DN_DV:])
            lhs_o = jnp.concatenate([a_pk[(g, c)][:, sl], da_pk[(g, c)][:, sl]], axis=1).astype(BF16)
            o_s[r0:r1, h0 * DN_DV:(h1 + 1) * DN_DV] = _dot(lhs_o, jnp.concatenate([bdv, bdq], axis=0))
            upd = _dot_tn(kd_all[(c, p)], bdv)
            d0 = jnp.exp(gc[r1 - 1:r1, h0:h0 + 1])
            d1 = jnp.exp(gc[r1 - 1:r1, h1:h1 + 1])
            s_mat[h0] = d0 * s_mat[h0] + upd[:, 0:DN_DV]
            s_mat[h1] = d1 * s_mat[h1] + upd[:, DN_DV:]
        issue_slab()
    for hd in range(DN_HEADS):
        state_s[hd] = s_mat[hd]
    issue_slab(len(queue))
    late = take(OFF_BZ, N_MAIN - OFF_BZ)
    z_b = _silu_of_half(late[:, 0:DN_HEADS * DN_DV])
    gates = late[:, DN_HEADS * DN_DV:]

    for hd in range(DN_HEADS):
        oh = o_s[:, hd * DN_DV:(hd + 1) * DN_DV]
        oh = oh * lax.rsqrt(jnp.mean(oh * oh, axis=-1, keepdims=True) + NORM_EPS) * ong_ref[...]
        o_s[:, hd * DN_DV:(hd + 1) * DN_DV] = oh
    y_b = (o_s[...] * z_b).astype(BF16)
    merged = _gate2_of_half(gates[:, D_MODEL:2 * D_MODEL]) * _dot(y_b, _weight(bproj_ref))
    merged = merged + _gate2_of_half(gates[:, 0:D_MODEL]) * _mixer_a(
        pa, ahist_s, adw_ref, adwb_ref, alng_ref, alnb_ref, aproj_ref)
    merged = merged + _gate2_of_half(gates[:, 2 * D_MODEL:]) * _mixer_c(
        pc, clng_ref, clnb_ref, cws_ref, cbst_ref, cproj_ref)

    y = x + _dot(merged.astype(BF16), _weight(wout_ref))
    if final_norm:
        y = y * lax.rsqrt(jnp.mean(y * y, axis=-1, keepdims=True) + NORM_EPS) * fg_ref[...]
    out_ref[0] = y


def _const_spec(shape):
    nd = len(shape)
    return pl.BlockSpec(shape, lambda b, t: (0,) * nd, pipeline_mode=pl.Buffered(1))


def _layer_call(x, params, final_g, final_norm):
    bsz, seq, d = x.shape
    assert d == D_MODEL and seq % SEQ_TILE == 0
    consts = list(params) + [final_g]
    tile_spec = pl.BlockSpec((1, SEQ_TILE, D_MODEL), lambda b, t: (b, t, 0))
    return pl.pallas_call(
        functools.partial(_layer_kernel, final_norm=final_norm),
        grid=(bsz, seq // SEQ_TILE),
        in_specs=[tile_spec] + [_const_spec(c.shape) for c in consts],
        out_specs=tile_spec,
        out_shape=jax.ShapeDtypeStruct(x.shape, F32),
        scratch_shapes=[
            pltpu.VMEM((SEQ_TILE + A_HALO, CONV_WIDTH), F32),
            pltpu.VMEM((SEQ_TILE + B_HALO, _QKV), F32),
            pltpu.VMEM((DN_HEADS, DN_DK, DN_DV), F32),
            pltpu.VMEM((SEQ_TILE, DN_HEADS * DN_DV), F32),
        ],
        compiler_params=pltpu.CompilerParams(
            dimension_semantics=("arbitrary", "arbitrary"),
            vmem_limit_bytes=VMEM_LIMIT_BYTES),
    )(x, *consts)


def _pack_rows(w):
    k = w.shape[0]
    order = np.concatenate([np.arange(0, k, 2), np.arange(1, k, 2)])
    select = jnp.asarray(np.eye(k, dtype=np.float32)[order], dtype=BF16)
    split = jnp.dot(select, w.astype(BF16), preferred_element_type=F32)
    bits = lax.bitcast_convert_type(split.astype(BF16), jnp.uint16).astype(jnp.uint32)
    return bits[:k // 2] | (bits[k // 2:] << 16)


def _prep_layer(norm_g, w_in, a_dw, a_dw_b, a_ln_g, a_ln_b, a_proj, b_conv, b_a_log, b_dt_bias,
                b_onorm_g, b_proj, c_ln_g, c_ln_b, c_ws, c_bs, c_proj, w_out):
    o = 0
    pieces = {}
    for name, n in (("a", 3 * CONV_WIDTH), ("qkv", _QKV), ("bz", DN_HEADS * DN_DV),
                    ("ba", 2 * DN_HEADS), ("c", 3 * SG_WIDTH), ("g", 3 * D_MODEL)):
        pieces[name] = w_in[:, o:o + n]
        o += n
    c_half = jnp.concatenate([pieces["c"][:, 0:2 * SG_WIDTH], 0.5 * pieces["c"][:, 2 * SG_WIDTH:]], axis=1)
    w_main = _pack_rows(jnp.concatenate([0.5 * pieces["a"], pieces["qkv"], c_half, 0.5 * pieces["bz"],
                                         0.5 * pieces["g"]], axis=1))
    row = lambda v: v.reshape(1, -1)
    return (row(norm_g), w_main, _pack_rows(pieces["ba"]),
            a_dw, row(a_dw_b), row(0.5 * a_ln_g), row(0.5 * a_ln_b), _pack_rows(0.5 * a_proj),
            0.5 * b_conv, row(b_a_log), row(b_dt_bias), row(b_onorm_g), _pack_rows(0.5 * b_proj),
            row(c_ln_g), row(c_ln_b), c_ws, c_bs.T, _pack_rows(0.5 * c_proj), _pack_rows(w_out))


def kernel(x, norm_g, w_in, a_dw, a_dw_b, a_ln_g, a_ln_b, a_proj, b_conv, b_a_log, b_dt_bias,
           b_onorm_g, b_proj, c_ln_g, c_ln_b, c_ws, c_bs, c_proj, w_out, final_g):
    depth = w_in.shape[0]
    fg = final_g.reshape(1, -1)
    for l in range(depth):
        params = _prep_layer(norm_g[l], w_in[l], a_dw[l], a_dw_b[l], a_ln_g[l], a_ln_b[l], a_proj[l],
                             b_conv[l], b_a_log[l], b_dt_bias[l], b_onorm_g[l], b_proj[l],
                             c_ln_g[l], c_ln_b[l], c_ws[l], c_bs[l], c_proj[l], w_out[l])
        x = _layer_call(x, params, fg, final_norm=(l == depth - 1))
    return x
```
